```python
import math
import jax, jax.numpy as jnp
from jax import lax
import numpy as np

D_MODEL = 1024
BATCH = 8
SEQ = 4096
DEPTH = 2

HEAD_DIM = 64
QBLOCK = 128
BRANCH_WIDTH = D_MODEL // 2
DIFF_HEADS = D_MODEL // 256
DIFF_V_DIM = 2 * HEAD_DIM
DIL_HEADS = D_MODEL // 128
DIL_PATTERNS = ((128, 1), (512, 4), (2048, 16))
SB_HEADS = D_MODEL // 128
N_BRANCHES = 3
IN_WIDTH = 9 * BRANCH_WIDTH
D_FF_DENSE = 256 * ((8 * D_MODEL // 3 + 255) // 256)
N_EXPERTS = 8
TOP_K = 2
D_FF_EXPERT = 7 * D_MODEL // 2
N_DENSE = (DEPTH + 1) // 2
N_MOE = DEPTH // 2
NORM_EPS = 1e-6
NEG_INF = -1e30

kernel_name = "hybrid_diff_dilated_stickbreak_moe"


def rmsnorm(x, g):
    xf = x.astype(jnp.float32)
    y = xf * lax.rsqrt(jnp.mean(xf * xf, axis=-1, keepdims=True) + NORM_EPS)
    return (y * g.astype(jnp.float32)).astype(x.dtype)


def alibi_slopes():
    n = DIFF_HEADS + DIL_HEADS
    slopes = (2.0 ** (-8.0 * np.arange(1, n + 1) / n)).astype(np.float32)
    is_diff = (np.arange(n) % 3) == 2
    return slopes[is_diff], slopes[~is_diff]


def diff_attention(q, k, v, lam, slopes):
    S, Dh = q.shape[3], q.shape[4]
    scale = Dh ** -0.5
    outs = []
    for b0 in range(0, S, QBLOCK):
        kl = b0 + QBLOCK
        s = jnp.einsum('bhcqd,bhckd->bhcqk', q[:, :, :, b0:kl], k[:, :, :, :kl]).astype(jnp.float32) * scale
        dist = np.arange(b0, kl)[:, None] - np.arange(kl)[None, :]
        bias = -slopes[:, None, None] * dist.astype(np.float32)[None]
        s = jnp.where(dist >= 0, s + bias[:, None], NEG_INF)
        p = jax.nn.softmax(s, axis=-1)
        a = p[:, :, 0] - lam * p[:, :, 1]
        outs.append(jnp.einsum('bhqk,bhkd->bhqd', a.astype(v.dtype), v[:, :, :kl]))
    return jnp.concatenate(outs, axis=2)


def dilated_attention(q, k, v, slopes):
    B, S, H, Dh = q.shape
    scale = Dh ** -0.5
    outs, maxes, denoms = [], [], []
    for window, dil in DIL_PATTERNS:
        span = window // dil
        n = S // dil
        qb = min(QBLOCK, n)
        nb = -(-n // qb)
        npad = nb * qb

        def to_sub(t):
            return t.reshape(B, n, dil, H, Dh).transpose(0, 2, 3, 1, 4)

        def from_sub(t):
            c = t.shape[-1]
            t = t.reshape(B, dil, H, npad, c)[:, :, :, :n]
            return t.transpose(0, 3, 1, 2, 4).reshape(B, S, H, c)

        qs = jnp.pad(to_sub(q), ((0, 0), (0, 0), (0, 0), (0, npad - n), (0, 0))).reshape(B, dil, H, nb, qb, Dh)
        kv_pad = ((0, 0), (0, 0), (0, 0), (span, npad - n), (0, 0))
        ks = jnp.pad(to_sub(k), kv_pad)
        vs = jnp.pad(to_sub(v), kv_pad)
        kidx = np.arange(nb)[:, None] * qb + np.arange(qb + span)[None, :]
        kg = ks[:, :, :, kidx]
        vg = vs[:, :, :, kidx]
        s = jnp.einsum('brhnqd,brhnkd->brhnqk', qs, kg).astype(jnp.float32) * scale
        qi = np.arange(qb)[:, None]
        kj = np.arange(qb + span)[None, :]
        delta = qi - kj + span
        key_sub = np.arange(nb)[:, None, None] * qb - span + kj[None]
        valid = (delta >= 0) & (delta <= span) & (key_sub >= 0)
        bias = -slopes[:, None, None, None] * (delta * dil).astype(np.float32)[None, None]
        s = jnp.where(valid, s + bias, NEG_INF)
        m = jnp.max(s, axis=-1, keepdims=True)
        p = jnp.exp(s - m)
        l = jnp.sum(p, axis=-1, keepdims=True)
        o = jnp.einsum('brhnqk,brhnkd->brhnqd', p.astype(v.dtype), vg).astype(jnp.float32)
        outs.append(from_sub(o))
        maxes.append(from_sub(m))
        denoms.append(from_sub(l))
    m_all = jnp.maximum(jnp.maximum(maxes[0], maxes[1]), maxes[2])
    num = 0.0
    den = 0.0
    for o, m, l in zip(outs, maxes, denoms):
        w = jnp.exp(m - m_all)
        num = num + w * o
        den = den + w * l
    return (num / den).astype(q.dtype)


def stick_breaking_attention(q, k, v):
    S, Dh = q.shape[2], q.shape[3]
    scale = Dh ** -0.5
    outs = []
    for b0 in range(0, S, QBLOCK):
        kl = b0 + QBLOCK
        z = jnp.einsum('bhqd,bhkd->bhqk', q[:, :, b0:kl], k[:, :, :kl]).astype(jnp.float32) * scale
        causal = (np.arange(b0, kl)[:, None] - np.arange(kl)[None, :]) > 0
        log_beta = jax.nn.log_sigmoid(z)
        log_1m = jnp.where(causal, jax.nn.log_sigmoid(-z), 0.0)
        rem = lax.cumsum(log_1m, axis=3, reverse=True) - log_1m
        a = jnp.where(causal, jnp.exp(log_beta + rem), 0.0)
        outs.append(jnp.einsum('bhqk,bhkd->bhqd', a.astype(v.dtype), v[:, :, :kl]))
    return jnp.concatenate(outs, axis=2)


def swiglu(h, w_g, w_u, w_d):
    return (jax.nn.silu(h @ w_g) * (h @ w_u)) @ w_d


def moe_swiglu(h, w_router, w_g, w_u, w_d):
    logits = (h @ w_router).astype(jnp.float32)
    top_val, top_idx = lax.top_k(logits, TOP_K)
    top_w = jax.nn.softmax(top_val, axis=-1)
    gate = jnp.sum(jax.nn.one_hot(top_idx, N_EXPERTS, dtype=jnp.float32) * top_w[..., None], axis=-2)
    out = jnp.zeros_like(h)
    for e in range(N_EXPERTS):
        out = out + gate[..., e:e + 1].astype(h.dtype) * swiglu(h, w_g[e], w_u[e], w_d[e])
    return out


def setup_inputs(seed: int = 0) -> dict:
    key = jax.random.key(seed)
    ks = jax.random.split(key, 20)
    D, W = D_MODEL, BRANCH_WIDTH
    f32 = jnp.float32

    def nrm(k, shape, fan_in):
        return jax.random.normal(k, shape, f32) * (fan_in ** -0.5)

    return {
        "x": jax.random.normal(ks[0], (BATCH, SEQ, D), f32),
        "norm_mix": 1.0 + 0.05 * jax.random.normal(ks[1], (DEPTH, D), f32),
        "w_in": nrm(ks[2], (DEPTH, D, IN_WIDTH), D),
        "diff_lambda": 0.1 * jax.random.normal(ks[3], (DEPTH, 4, HEAD_DIM), f32),
        "diff_subln": 1.0 + 0.05 * jax.random.normal(ks[4], (DEPTH, DIFF_V_DIM), f32),
        "w_gate": nrm(ks[5], (DEPTH, D, N_BRANCHES * D), D),
        "b_gate": 0.02 * jax.random.normal(ks[6], (DEPTH, N_BRANCHES * D), f32),
        "w_branch": nrm(ks[7], (DEPTH, N_BRANCHES, W, D), W),
        "w_out": nrm(ks[8], (DEPTH, D, D), D),
        "norm_ffn": 1.0 + 0.05 * jax.random.normal(ks[9], (DEPTH, D), f32),
        "w_dense_gate": nrm(ks[10], (N_DENSE, D, D_FF_DENSE), D),
        "w_dense_up": nrm(ks[11], (N_DENSE, D, D_FF_DENSE), D),
        "w_dense_down": nrm(ks[12], (N_DENSE, D_FF_DENSE, D), D_FF_DENSE),
        "w_router": nrm(ks[13], (N_MOE, D, N_EXPERTS), D),
        "w_moe_gate": nrm(ks[14], (N_MOE, N_EXPERTS, D, D_FF_EXPERT), D),
        "w_moe_up": nrm(ks[15], (N_MOE, N_EXPERTS, D, D_FF_EXPERT), D),
        "w_moe_down": nrm(ks[16], (N_MOE, N_EXPERTS, D_FF_EXPERT, D), D_FF_EXPERT),
        "norm_final": 1.0 + 0.05 * jax.random.normal(ks[17], (D,), f32),
    }


def reference(x, norm_mix, w_in, diff_lambda, diff_subln, w_gate, b_gate, w_branch, w_out,
              norm_ffn, w_dense_gate, w_dense_up, w_dense_down,
              w_router, w_moe_gate, w_moe_up, w_moe_down, norm_final):
    B, S, D = x.shape
    W = BRANCH_WIDTH
    slopes_diff, slopes_dil = alibi_slopes()
    for layer in range(DEPTH):
        h = rmsnorm(x, norm_mix[layer])
        proj = h @ w_in[layer]
        qa, ka, va, qb, kb, vb, qc, kc, vc = jnp.split(proj, 9, axis=-1)

        qa = qa.reshape(B, S, DIFF_HEADS, 2, HEAD_DIM).transpose(0, 2, 3, 1, 4)
        ka = ka.reshape(B, S, DIFF_HEADS, 2, HEAD_DIM).transpose(0, 2, 3, 1, 4)
        va = va.reshape(B, S, DIFF_HEADS, DIFF_V_DIM).transpose(0, 2, 1, 3)
        lam_init = 0.8 - 0.6 * math.exp(-0.3 * layer)
        lv = diff_lambda[layer].astype(jnp.float32)
        lam = jnp.exp(jnp.sum(lv[0] * lv[1])) - jnp.exp(jnp.sum(lv[2] * lv[3])) + lam_init
        ya = diff_attention(qa, ka, va, lam, slopes_diff)
        ya = rmsnorm(ya, diff_subln[layer]) * (1.0 - lam_init)
        ya = ya.transpose(0, 2, 1, 3).reshape(B, S, W)

        yb = dilated_attention(qb.reshape(B, S, DIL_HEADS, HEAD_DIM),
                               kb.reshape(B, S, DIL_HEADS, HEAD_DIM),
                               vb.reshape(B, S, DIL_HEADS, HEAD_DIM), slopes_dil).reshape(B, S, W)

        to_heads = lambda t: t.reshape(B, S, SB_HEADS, HEAD_DIM).transpose(0, 2, 1, 3)
        yc = stick_breaking_attention(to_heads(qc), to_heads(kc), to_heads(vc))
        yc = yc.transpose(0, 2, 1, 3).reshape(B, S, W)

        gates = jax.nn.sigmoid(h @ w_gate[layer] + b_gate[layer])
        merged = jnp.zeros_like(x)
        for i, y in enumerate((ya, yb, yc)):
            merged = merged + gates[..., i * D:(i + 1) * D] * (y @ w_branch[layer, i])
        x = x + merged @ w_out[layer]

        h = rmsnorm(x, norm_ffn[layer])
        j = layer // 2
        if layer % 2 == 0:
            f = swiglu(h, w_dense_gate[j], w_dense_up[j], w_dense_down[j])
        else:
            f = moe_swiglu(h, w_router[j], w_moe_gate[j], w_moe_up[j], w_moe_down[j])
        x = x + f
    return rmsnorm(x, norm_final)
```

```python
import functools
import math

import numpy as np
import jax
import jax.numpy as jnp
from jax import lax
from jax.experimental import pallas as pl
from jax.experimental.pallas import tpu as pltpu

F32 = jnp.float32
BF16 = jnp.bfloat16

HEAD_DIM = 64
QK_SCALE = HEAD_DIM ** -0.5
DIFF_HEADS = 4
DIL_HEADS = 8
SB_HEADS = 8
DIL_PATTERNS = ((128, 1), (512, 4), (2048, 16))
DIL_SPAN = 128
N_EXPERTS = 8
NORM_EPS = 1e-6
NEG_INF = -1e30
SB_SKIP_LOG = -104.0
LANES = 128
VMEM_LIMIT = 56 * 1024 * 1024

_NT = (((1,), (1,)), ((), ()))


def _alibi_slopes():
    n = DIFF_HEADS + DIL_HEADS
    slopes = (2.0 ** (-8.0 * np.arange(1, n + 1) / n)).astype(np.float32)
    is_diff = (np.arange(n) % 3) == 2
    return slopes[is_diff], slopes[~is_diff]


def _rms(x, g):
    ms = jnp.mean(x * x, axis=-1, keepdims=True)
    return x * lax.rsqrt(ms + NORM_EPS) * g


def _dot(a, b):
    return jnp.dot(a, b, preferred_element_type=F32)


def _sigmoid(a):
    return 1.0 / (1.0 + jnp.exp(-a))


def _params(*sem):
    return pltpu.CompilerParams(dimension_semantics=sem, vmem_limit_bytes=VMEM_LIMIT)


def _const_spec(shape):
    nd = len(shape)
    return pl.BlockSpec(shape, lambda *_: (0,) * nd)


def _norm_proj_kernel(x_ref, g_ref, w_ref, o_ref, *, n_chunk):
    h = _rms(x_ref[...], g_ref[...]).astype(BF16)
    for n0 in range(0, o_ref.shape[1], n_chunk):
        o_ref[:, n0:n0 + n_chunk] = _dot(h, w_ref[:, n0:n0 + n_chunk]).astype(o_ref.dtype)


def _norm_proj(x, g, w, tm=512, n_chunk=512):
    T, D = x.shape
    N = w.shape[1]
    return pl.pallas_call(
        functools.partial(_norm_proj_kernel, n_chunk=n_chunk),
        grid=(T // tm,),
        in_specs=[pl.BlockSpec((tm, D), lambda i: (i, 0)), _const_spec((1, D)), _const_spec((D, N))],
        out_specs=pl.BlockSpec((tm, N), lambda i: (i, 0)),
        out_shape=jax.ShapeDtypeStruct((T, N), BF16),
        compiler_params=_params("parallel"),
        name="norm_proj",
    )(x, g.reshape(1, D), w)


def _diff_attn_kernel(nslope_ref, q_ref, k_ref, v_ref, lam_ref, subln_ref, o_ref, m_sc, l_sc, acc_sc, *, tq, lam_init):
    i = pl.program_id(2)
    nslope = nslope_ref[pl.program_id(1)]
    ii = lax.broadcasted_iota(jnp.int32, (tq, tq), 0)
    jj = lax.broadcasted_iota(jnp.int32, (tq, tq), 1)
    rel = (ii - jj).astype(F32) * nslope
    causal = ii >= jj
    q = q_ref[...] * jnp.asarray(QK_SCALE, BF16)
    m_sc[...] = jnp.full(m_sc.shape, NEG_INF, F32)
    l_sc[...] = jnp.zeros(l_sc.shape, F32)
    acc_sc[...] = jnp.zeros(acc_sc.shape, F32)

    def tile(j, masked):
        r0 = pl.multiple_of(j * tq, tq)
        k = k_ref[pl.ds(r0, tq), :]
        v = v_ref[pl.ds(r0, tq), :]
        off = jnp.full((tq, 1), (i - j) * tq, jnp.int32).astype(F32) * nslope
        for c in range(2):
            s = lax.dot_general(q[:, c * HEAD_DIM:(c + 1) * HEAD_DIM], k[:, c * HEAD_DIM:(c + 1) * HEAD_DIM], _NT,
                                preferred_element_type=F32)
            t = s + rel
            if masked:
                t = jnp.where(causal, t, NEG_INF)
            m_prev = m_sc[c]
            m_new = jnp.maximum(m_prev, jnp.max(t, axis=1, keepdims=True) + off)
            alpha = jnp.exp(m_prev - m_new)
            p = jnp.exp(t - (m_new - off))
            l_sc[c] = alpha * l_sc[c] + jnp.sum(p, axis=1, keepdims=True)
            acc_sc[c] = alpha * acc_sc[c] + _dot(p.astype(BF16), v)
            m_sc[c] = m_new

    def body(j, carry):
        tile(j, False)
        return carry

    lax.fori_loop(0, i, body, 0)
    tile(i, True)

    lv = lam_ref[...]
    lam = (jnp.exp(jnp.sum(lv[0:1] * lv[1:2], axis=1, keepdims=True))
           - jnp.exp(jnp.sum(lv[2:3] * lv[3:4], axis=1, keepdims=True)) + lam_init)
    o = acc_sc[0] / l_sc[0] - lam * (acc_sc[1] / l_sc[1])
    o_ref[...] = (_rms(o, subln_ref[...]) * (1.0 - lam_init)).astype(o_ref.dtype)


def _diff_attn(proj, lam_vec, subln, nslopes, B, S, lam_init, tq=256):
    T = B * S
    nq = S // tq
    W = DIFF_HEADS * 2 * HEAD_DIM
    nh = DIFF_HEADS
    return pl.pallas_call(
        functools.partial(_diff_attn_kernel, tq=tq, lam_init=lam_init),
        grid=(B, nh, nq),
        in_specs=[
            pl.BlockSpec(memory_space=pltpu.SMEM),
            pl.BlockSpec((tq, LANES), lambda b, h, i: (b * nq + i, h)),
            pl.BlockSpec((S, LANES), lambda b, h, i: (b, nh + h)),
            pl.BlockSpec((S, LANES), lambda b, h, i: (b, 2 * nh + h)),
            _const_spec((4, HEAD_DIM)),
            _const_spec((1, 2 * HEAD_DIM)),
        ],
        out_specs=pl.BlockSpec((tq, LANES), lambda b, h, i: (b * nq + i, h)),
        out_shape=jax.ShapeDtypeStruct((T, W), BF16),
        scratch_shapes=[pltpu.VMEM((2, tq, 1), F32), pltpu.VMEM((2, tq, 1), F32), pltpu.VMEM((2, tq, LANES), F32)],
        compiler_params=_params("parallel", "parallel", "arbitrary"),
        name="diff_attn",
    )(nslopes, proj, proj, proj, lam_vec, subln.reshape(1, -1))


def _dil_attn_kernel(q_ref, kp_ref, kc_ref, vp_ref, vc_ref, o_ref, lse_ref, *, dil, slopes):
    i = pl.program_id(2)
    tq = DIL_SPAN
    qi = lax.broadcasted_iota(jnp.int32, (tq, 2 * tq), 0)
    kj = lax.broadcasted_iota(jnp.int32, (tq, 2 * tq), 1)
    delta = qi - kj + DIL_SPAN
    first_key = jnp.where(i > 0, 0, tq)
    valid = (delta >= 0) & (delta <= DIL_SPAN) & (kj >= first_key)
    deltaf = delta.astype(F32)
    q = q_ref[...] * jnp.asarray(QK_SCALE, BF16)
    k = jnp.concatenate([kp_ref[...], kc_ref[...]], axis=0)
    v = jnp.concatenate([vp_ref[...], vc_ref[...]], axis=0)
    for h in range(DIL_HEADS):
        cs = slice(h * HEAD_DIM, (h + 1) * HEAD_DIM)
        s = lax.dot_general(q[:, cs], k[:, cs], _NT, preferred_element_type=F32)
        s = jnp.where(valid, s + deltaf * float(-slopes[h] * dil), NEG_INF)
        m = jnp.max(s, axis=1, keepdims=True)
        p = jnp.exp(s - m)
        l = jnp.sum(p, axis=1, keepdims=True)
        o_ref[:, cs] = (_dot(p.astype(BF16), v[:, cs]) / l).astype(o_ref.dtype)
        lse_ref[:, cs] = jnp.broadcast_to(m + jnp.log(l), (tq, HEAD_DIM))


def _dil_attn(proj, slopes, B, S, dil, in_width):
    T = B * S
    tq = DIL_SPAN
    n = S // dil
    nq = n // tq
    W = DIL_HEADS * HEAD_DIM
    cpr = in_width // W
    qcol = (3 * W) // W
    view = proj.reshape(T // dil, dil * in_width)
    kv_prev = lambda b, r, i: b * nq + jnp.maximum(i - 1, 0)
    in_specs = [
        pl.BlockSpec((tq, W), lambda b, r, i: (b * nq + i, r * cpr + qcol)),
        pl.BlockSpec((tq, W), lambda b, r, i: (kv_prev(b, r, i), r * cpr + qcol + 1)),
        pl.BlockSpec((tq, W), lambda b, r, i: (b * nq + i, r * cpr + qcol + 1)),
        pl.BlockSpec((tq, W), lambda b, r, i: (kv_prev(b, r, i), r * cpr + qcol + 2)),
        pl.BlockSpec((tq, W), lambda b, r, i: (b * nq + i, r * cpr + qcol + 2)),
    ]
    out_spec = pl.BlockSpec((tq, W), lambda b, r, i: (b * nq + i, r))
    o, lse = pl.pallas_call(
        functools.partial(_dil_attn_kernel, dil=dil, slopes=tuple(float(s) for s in slopes)),
        grid=(B, dil, nq),
        in_specs=in_specs,
        out_specs=[out_spec, out_spec],
        out_shape=[jax.ShapeDtypeStruct((T // dil, dil * W), BF16), jax.ShapeDtypeStruct((T // dil, dil * W), F32)],
        compiler_params=_params("parallel", "parallel", "arbitrary"),
        name=f"dil_attn_d{dil}",
    )(view, view, view, view, view)
    return o.reshape(T, W), lse.reshape(T, W)


def _sb_attn_kernel(q_ref, k_ref, v_ref, o_ref, c_sc, acc_sc, *, tq):
    i = pl.program_id(2)
    ii = lax.broadcasted_iota(jnp.int32, (tq, tq), 0)
    jj = lax.broadcasted_iota(jnp.int32, (tq, tq), 1)
    strict = ii > jj
    upper = jnp.where(strict, 1.0, 0.0).astype(BF16)
    q = q_ref[...] * jnp.asarray(QK_SCALE, BF16)

    for hh in range(2):
        cs = slice(hh * HEAD_DIM, (hh + 1) * HEAD_DIM)
        qh = q[:, cs]

        def tile(j, masked, hh=hh, cs=cs, qh=qh):
            r0 = pl.multiple_of(j * tq, tq)
            k = k_ref[pl.ds(r0, tq), cs]
            v = v_ref[pl.ds(r0, tq), cs]
            z = lax.dot_general(qh, k, _NT, preferred_element_type=F32)
            sp = jnp.maximum(z, 0.0) + jnp.log(1.0 + jnp.exp(-jnp.abs(z)))
            log_1m = -sp
            if masked:
                log_1m = jnp.where(strict, log_1m, 0.0)
            hi = log_1m.astype(BF16)
            lo = (log_1m - hi.astype(F32)).astype(BF16)
            rem = _dot(hi, upper) + _dot(lo, upper)
            c = c_sc[hh]
            a = jnp.exp(z - sp + rem + c)
            if masked:
                a = jnp.where(strict, a, 0.0)
            acc_sc[hh] = acc_sc[hh] + _dot(a.astype(BF16), v)
            c_sc[hh] = c + jnp.sum(log_1m, axis=1, keepdims=True)

        c_sc[hh] = jnp.zeros((tq, 1), F32)
        acc_sc[hh] = jnp.zeros((tq, HEAD_DIM), F32)
        tile(i, True)

        def cond(j, hh=hh):
            return jnp.logical_and(j >= 0, jnp.max(c_sc[hh]) >= SB_SKIP_LOG)

        def body(j, tile=tile):
            tile(j, False)
            return j - 1

        lax.while_loop(cond, body, i - 1)
        o_ref[:, cs] = acc_sc[hh].astype(o_ref.dtype)


def _sb_attn(proj, B, S, tq=256):
    T = B * S
    nq = S // tq
    W = SB_HEADS * HEAD_DIM
    nb = W // LANES
    base = 6 * nb
    return pl.pallas_call(
        functools.partial(_sb_attn_kernel, tq=tq),
        grid=(B, nb, nq),
        in_specs=[
            pl.BlockSpec((tq, LANES), lambda b, h, i: (b * nq + i, base + h)),
            pl.BlockSpec((S, LANES), lambda b, h, i: (b, base + nb + h)),
            pl.BlockSpec((S, LANES), lambda b, h, i: (b, base + 2 * nb + h)),
        ],
        out_specs=pl.BlockSpec((tq, LANES), lambda b, h, i: (b * nq + i, h)),
        out_shape=jax.ShapeDtypeStruct((T, W), BF16),
        scratch_shapes=[pltpu.VMEM((2, tq, 1), F32), pltpu.VMEM((2, tq, HEAD_DIM), F32)],
        compiler_params=_params("parallel", "parallel", "arbitrary"),
        name="sb_attn",
    )(proj, proj, proj)


def _merge_kernel(x_ref, g_ref, ya_ref, o1_ref, o2_ref, o3_ref, l1_ref, l2_ref, l3_ref, yc_ref,
                  wg_ref, bg_ref, wb_ref, wo_ref, out_ref, *, n_chunk):
    x = x_ref[...]
    D = x.shape[1]
    h = _rms(x, g_ref[...]).astype(BF16)
    l1, l2, l3 = l1_ref[...], l2_ref[...], l3_ref[...]
    mx = jnp.maximum(jnp.maximum(l1, l2), l3)
    w1, w2, w3 = jnp.exp(l1 - mx), jnp.exp(l2 - mx), jnp.exp(l3 - mx)
    yb = ((w1 * o1_ref[...] + w2 * o2_ref[...] + w3 * o3_ref[...]) / (w1 + w2 + w3)).astype(BF16)
    ys = (ya_ref[...], yb, yc_ref[...])
    parts = []
    for c0 in range(0, D, n_chunk):
        m = jnp.zeros((x.shape[0], n_chunk), F32)
        for i in range(3):
            gc = slice(i * D + c0, i * D + c0 + n_chunk)
            gate = _sigmoid(_dot(h, wg_ref[:, gc]) + bg_ref[:, gc])
            m = m + gate * _dot(ys[i], wb_ref[i, :, c0:c0 + n_chunk])
        parts.append(m.astype(BF16))
    out_ref[...] = x + _dot(jnp.concatenate(parts, axis=1), wo_ref[...])


def _merge(x, g, ya, dil_outs, yc, w_gate, b_gate, w_branch, w_out, tm=256, n_chunk=512):
    T, D = x.shape
    W = ya.shape[1]
    row = lambda i: (i, 0)
    ysp = pl.BlockSpec((tm, W), row)
    (o1, s1), (o2, s2), (o3, s3) = dil_outs
    return pl.pallas_call(
        functools.partial(_merge_kernel, n_chunk=n_chunk),
        grid=(T // tm,),
        in_specs=[pl.BlockSpec((tm, D), row), _const_spec((1, D)), ysp, ysp, ysp, ysp, ysp, ysp, ysp, ysp,
                  _const_spec(w_gate.shape), _const_spec((1, 3 * D)), _const_spec(w_branch.shape),
                  _const_spec(w_out.shape)],
        out_specs=pl.BlockSpec((tm, D), row),
        out_shape=jax.ShapeDtypeStruct((T, D), F32),
        compiler_params=_params("parallel"),
        name="merge",
    )(x, g.reshape(1, D), ya, o1, o2, o3, s1, s2, s3, yc, w_gate, b_gate.reshape(1, -1), w_branch, w_out)


def _ffn_dense_kernel(x_ref, g_ref, wg_ref, wu_ref, wd_ref, gf_ref, out_ref, *, f_chunk, final):
    x = x_ref[...]
    h = _rms(x, g_ref[...]).astype(BF16)
    acc = x
    for f0 in range(0, wg_ref.shape[1], f_chunk):
        a = _dot(h, wg_ref[:, f0:f0 + f_chunk])
        u = _dot(h, wu_ref[:, f0:f0 + f_chunk])
        acc = acc + _dot((a * _sigmoid(a) * u).astype(BF16), wd_ref[f0:f0 + f_chunk, :])
    out_ref[...] = _rms(acc, gf_ref[...]) if final else acc


def _ffn_dense(x, g, w_g, w_u, w_d, g_final, final, tm=512, f_chunk=256):
    T, D = x.shape
    row = lambda i: (i, 0)
    return pl.pallas_call(
        functools.partial(_ffn_dense_kernel, f_chunk=f_chunk, final=final),
        grid=(T // tm,),
        in_specs=[pl.BlockSpec((tm, D), row), _const_spec((1, D)), _const_spec(w_g.shape), _const_spec(w_u.shape),
                  _const_spec(w_d.shape), _const_spec((1, D))],
        out_specs=pl.BlockSpec((tm, D), row),
        out_shape=jax.ShapeDtypeStruct((T, D), F32),
        compiler_params=_params("parallel"),
        name="ffn_dense",
    )(x, g.reshape(1, D), w_g, w_u, w_d, g_final.reshape(1, D))


def _moe_kernel(x_ref, g_ref, wr_ref, wg_ref, wu_ref, wd_ref, gf_ref, out_ref, h_sc, gate_sc, acc_sc, *, final):
    e = pl.program_id(1)
    f = pl.program_id(2)
    lane = lax.broadcasted_iota(jnp.int32, gate_sc.shape, 1)

    @pl.when(jnp.logical_and(e == 0, f == 0))
    def _():
        x = x_ref[...]
        hf = _rms(x, g_ref[...])
        h_sc[...] = hf.astype(BF16)
        logits = jnp.dot(hf, wr_ref[...], preferred_element_type=F32, precision=lax.Precision.HIGHEST)
        logits = jnp.where(lane < N_EXPERTS, logits, NEG_INF)
        v1 = jnp.max(logits, axis=1, keepdims=True)
        i1 = jnp.min(jnp.where(logits == v1, lane, LANES), axis=1, keepdims=True)
        rest = jnp.where(lane == i1, NEG_INF, logits)
        v2 = jnp.max(rest, axis=1, keepdims=True)
        i2 = jnp.min(jnp.where(rest == v2, lane, LANES), axis=1, keepdims=True)
        e2 = jnp.exp(v2 - v1)
        gate_sc[...] = jnp.where(lane == i1, 1.0 / (1.0 + e2), jnp.where(lane == i2, e2 / (1.0 + e2), 0.0))
        acc_sc[...] = x

    h = h_sc[...]
    a = _dot(h, wg_ref[0])
    u = _dot(h, wu_ref[0])
    ge = jnp.sum(jnp.where(lane == e, gate_sc[...], 0.0), axis=1, keepdims=True)
    acc_sc[...] += _dot((a * _sigmoid(a) * u * ge).astype(BF16), wd_ref[0])

    @pl.when(jnp.logical_and(e == pl.num_programs(1) - 1, f == pl.num_programs(2) - 1))
    def _():
        acc = acc_sc[...]
        out_ref[...] = _rms(acc, gf_ref[...]) if final else acc


def _moe(x, g, w_router, w_g, w_u, w_d, g_final, final, tm=512, tf=1792):
    T, D = x.shape
    E, _, F = w_g.shape
    wr = jnp.zeros((D, LANES), F32).at[:, :E].set(w_router)
    row = lambda i, e, f: (i, 0)
    return pl.pallas_call(
        functools.partial(_moe_kernel, final=final),
        grid=(T // tm, E, F // tf),
        in_specs=[pl.BlockSpec((tm, D), row), _const_spec((1, D)), _const_spec((D, LANES)),
                  pl.BlockSpec((1, D, tf), lambda i, e, f: (e, 0, f)),
                  pl.BlockSpec((1, D, tf), lambda i, e, f: (e, 0, f)),
                  pl.BlockSpec((1, tf, D), lambda i, e, f: (e, f, 0)),
                  _const_spec((1, D))],
        out_specs=pl.BlockSpec((tm, D), row),
        out_shape=jax.ShapeDtypeStruct((T, D), F32),
        scratch_shapes=[pltpu.VMEM((tm, D), BF16), pltpu.VMEM((tm, LANES), F32), pltpu.VMEM((tm, D), F32)],
        compiler_params=_params("parallel", "arbitrary", "arbitrary"),
        name="moe",
    )(x, g.reshape(1, D), wr, w_g, w_u, w_d, g_final.reshape(1, D))


def kernel(x, norm_mix, w_in, diff_lambda, diff_subln, w_gate, b_gate, w_branch, w_out, norm_ffn, w_dense_gate,
           w_dense_up, w_dense_down, w_router, w_moe_gate, w_moe_up, w_moe_down, norm_final):
    B, S, D = x.shape
    depth = norm_mix.shape[0]
    in_width = w_in.shape[2]
    slopes_diff, slopes_dil = _alibi_slopes()
    nslopes_diff = jnp.asarray(-slopes_diff, F32)
    xf = x.reshape(B * S, D)
    for layer in range(depth):
        proj = _norm_proj(xf, norm_mix[layer], w_in[layer].astype(BF16))
        lam_init = 0.8 - 0.6 * math.exp(-0.3 * layer)
        ya = _diff_attn(proj, diff_lambda[layer], diff_subln[layer], nslopes_diff, B, S, lam_init)
        dil_outs = [_dil_attn(proj, slopes_dil, B, S, dil, in_width) for _, dil in DIL_PATTERNS]
        yc = _sb_attn(proj, B, S)
        xf = _merge(xf, norm_mix[layer], ya, dil_outs, yc, w_gate[layer].astype(BF16), b_gate[layer],
                    w_branch[layer].astype(BF16), w_out[layer].astype(BF16))
        final = layer == depth - 1
        j = layer // 2
        if layer % 2 == 0:
            xf = _ffn_dense(xf, norm_ffn[layer], w_dense_gate[j].astype(BF16), w_dense_up[j].astype(BF16),
                            w_dense_down[j].astype(BF16), norm_final, final)
        else:
            xf = _moe(xf, norm_ffn[layer], w_router[j], w_moe_gate[j].astype(BF16), w_moe_up[j].astype(BF16),
                      w_moe_down[j].astype(BF16), norm_final, final)
    return xf.reshape(B, S, D)
```

```python
import functools
import math

import numpy as np
import jax
import jax.numpy as jnp
from jax import lax
from jax.experimental import pallas as pl
from jax.experimental.pallas import tpu as pltpu

F32 = jnp.float32
BF16 = jnp.bfloat16

HEAD_DIM = 64
QK_SCALE = HEAD_DIM ** -0.5
DIFF_HEADS = 4
DIL_HEADS = 8
SB_HEADS = 8
DIL_PATTERNS = ((128, 1), (512, 4), (2048, 16))
DIL_SPAN = 128
N_EXPERTS = 8
NORM_EPS = 1e-6
NEG_INF = -1e30
SB_SKIP_LOG = -104.0
LANES = 128
VMEM_LIMIT = 56 * 1024 * 1024

_NT = (((1,), (1,)), ((), ()))


def _alibi_slopes():
    n = DIFF_HEADS + DIL_HEADS
    slopes = (2.0 ** (-8.0 * np.arange(1, n + 1) / n)).astype(np.float32)
    is_diff = (np.arange(n) % 3) == 2
    return slopes[is_diff], slopes[~is_diff]


def _rms(x, g):
    ms = jnp.mean(x * x, axis=-1, keepdims=True)
    return x * lax.rsqrt(ms + NORM_EPS) * g


def _dot(a, b):
    return jnp.dot(a, b, preferred_element_type=F32)


def _sigmoid(a):
    return 1.0 / (1.0 + jnp.exp(-a))


def _params(*sem):
    return pltpu.CompilerParams(dimension_semantics=sem, vmem_limit_bytes=VMEM_LIMIT)


def _const_spec(shape):
    nd = len(shape)
    return pl.BlockSpec(shape, lambda *_: (0,) * nd)


def _norm_proj_kernel(x_ref, g_ref, w_ref, *o_refs, n_chunk):
    h = _rms(x_ref[...], g_ref[...]).astype(BF16)
    col = 0
    for o_ref in o_refs:
        for n0 in range(0, o_ref.shape[1], n_chunk):
            o_ref[:, n0:n0 + n_chunk] = _dot(h, w_ref[:, col + n0:col + n0 + n_chunk]).astype(o_ref.dtype)
        col += o_ref.shape[1]


def _norm_proj(x, g, w, n_out, tm=512, n_chunk=512):
    T, D = x.shape
    N = w.shape[1] // n_out
    return pl.pallas_call(
        functools.partial(_norm_proj_kernel, n_chunk=n_chunk),
        grid=(T // tm,),
        in_specs=[pl.BlockSpec((tm, D), lambda i: (i, 0)), _const_spec((1, D)), _const_spec(w.shape)],
        out_specs=[pl.BlockSpec((tm, N), lambda i: (i, 0))] * n_out,
        out_shape=[jax.ShapeDtypeStruct((T, N), BF16)] * n_out,
        compiler_params=_params("parallel"),
        name="norm_proj",
    )(x, g.reshape(1, D), w)


def _diff_attn_kernel(nslope_ref, q_ref, k_ref, v_ref, lam_ref, subln_ref, o_ref, m_sc, l_sc, acc_sc, *, tq, lam_init):
    i = pl.program_id(2)
    nslope = nslope_ref[pl.program_id(1)]
    ii = lax.broadcasted_iota(jnp.int32, (tq, tq), 0)
    jj = lax.broadcasted_iota(jnp.int32, (tq, tq), 1)
    rel = (ii - jj).astype(F32) * nslope
    causal = ii >= jj
    q = q_ref[...] * jnp.asarray(QK_SCALE, BF16)
    lane = lax.broadcasted_iota(jnp.int32, q.shape, 1)
    zero = jnp.zeros_like(q)
    qs = (jnp.where(lane < HEAD_DIM, q, zero), jnp.where(lane >= HEAD_DIM, q, zero))
    m_sc[...] = jnp.full(m_sc.shape, NEG_INF, F32)
    l_sc[...] = jnp.zeros(l_sc.shape, F32)
    acc_sc[...] = jnp.zeros(acc_sc.shape, F32)

    def tile(j, masked):
        r0 = pl.multiple_of(j * tq, tq)
        k = k_ref[pl.ds(r0, tq), :]
        v = v_ref[pl.ds(r0, tq), :]
        off = jnp.full((tq, 1), (i - j) * tq, jnp.int32).astype(F32) * nslope
        for c in range(2):
            s = lax.dot_general(qs[c], k, _NT, preferred_element_type=F32)
            t = s + rel
            if masked:
                t = jnp.where(causal, t, NEG_INF)
            m_prev = m_sc[c]
            m_new = jnp.maximum(m_prev, jnp.max(t, axis=1, keepdims=True) + off)
            alpha = jnp.exp(m_prev - m_new)
            p = jnp.exp(t - (m_new - off))
            l_sc[c] = alpha * l_sc[c] + jnp.sum(p, axis=1, keepdims=True)
            acc_sc[c] = alpha * acc_sc[c] + _dot(p.astype(BF16), v)
            m_sc[c] = m_new

    def body(j, carry):
        tile(j, False)
        return carry

    lax.fori_loop(0, i, body, 0)
    tile(i, True)

    lv = lam_ref[...]
    lam = (jnp.exp(jnp.sum(lv[0:1] * lv[1:2], axis=1, keepdims=True))
           - jnp.exp(jnp.sum(lv[2:3] * lv[3:4], axis=1, keepdims=True)) + lam_init)
    o = acc_sc[0] / l_sc[0] - lam * (acc_sc[1] / l_sc[1])
    o_ref[...] = (_rms(o, subln_ref[...]) * (1.0 - lam_init)).astype(o_ref.dtype)


def _diff_attn(proj, lam_vec, subln, nslopes, B, S, lam_init, tq=512):
    T = B * S
    nq = S // tq
    W = DIFF_HEADS * 2 * HEAD_DIM
    nh = DIFF_HEADS
    return pl.pallas_call(
        functools.partial(_diff_attn_kernel, tq=tq, lam_init=lam_init),
        grid=(B, nh, nq),
        in_specs=[
            pl.BlockSpec(memory_space=pltpu.SMEM),
            pl.BlockSpec((tq, LANES), lambda b, h, i: (b * nq + i, h)),
            pl.BlockSpec((S, LANES), lambda b, h, i: (b, nh + h)),
            pl.BlockSpec((S, LANES), lambda b, h, i: (b, 2 * nh + h)),
            _const_spec((4, HEAD_DIM)),
            _const_spec((1, 2 * HEAD_DIM)),
        ],
        out_specs=pl.BlockSpec((tq, LANES), lambda b, h, i: (b * nq + i, h)),
        out_shape=jax.ShapeDtypeStruct((T, W), BF16),
        scratch_shapes=[pltpu.VMEM((2, tq, 1), F32), pltpu.VMEM((2, tq, 1), F32), pltpu.VMEM((2, tq, LANES), F32)],
        compiler_params=_params("parallel", "parallel", "arbitrary"),
        name="diff_attn",
    )(nslopes, proj, proj, proj, lam_vec, subln.reshape(1, -1))


def _dil_attn_kernel(q_ref, kp_ref, kc_ref, vp_ref, vc_ref, o_ref, lse_ref, *, dil, slopes):
    i = pl.program_id(2)
    tq = DIL_SPAN
    qi = lax.broadcasted_iota(jnp.int32, (tq, 2 * tq), 0)
    kj = lax.broadcasted_iota(jnp.int32, (tq, 2 * tq), 1)
    delta = qi - kj + DIL_SPAN
    first_key = jnp.where(i > 0, 0, tq)
    valid = (delta >= 0) & (delta <= DIL_SPAN) & (kj >= first_key)
    deltaf = delta.astype(F32)
    q = q_ref[...] * jnp.asarray(QK_SCALE, BF16)
    k = jnp.concatenate([kp_ref[...], kc_ref[...]], axis=0)
    v = jnp.concatenate([vp_ref[...], vc_ref[...]], axis=0)
    for h in range(DIL_HEADS):
        cs = slice(h * HEAD_DIM, (h + 1) * HEAD_DIM)
        s = lax.dot_general(q[:, cs], k[:, cs], _NT, preferred_element_type=F32)
        s = jnp.where(valid, s + deltaf * float(-slopes[h] * dil), NEG_INF)
        m = jnp.max(s, axis=1, keepdims=True)
        p = jnp.exp(s - m)
        l = jnp.sum(p, axis=1, keepdims=True)
        o_ref[:, cs] = (_dot(p.astype(BF16), v[:, cs]) / l).astype(o_ref.dtype)
        lse_ref[:, cs] = jnp.broadcast_to(m + jnp.log(l), (tq, HEAD_DIM))


def _dil_attn(proj, slopes, B, S, dil, in_width):
    T = B * S
    tq = DIL_SPAN
    n = S // dil
    nq = n // tq
    W = DIL_HEADS * HEAD_DIM
    cpr = in_width // W
    qcol = 0
    view = proj.reshape(T // dil, dil * in_width)
    kv_prev = lambda b, r, i: b * nq + jnp.maximum(i - 1, 0)
    in_specs = [
        pl.BlockSpec((tq, W), lambda b, r, i: (b * nq + i, r * cpr + qcol)),
        pl.BlockSpec((tq, W), lambda b, r, i: (kv_prev(b, r, i), r * cpr + qcol + 1)),
        pl.BlockSpec((tq, W), lambda b, r, i: (b * nq + i, r * cpr + qcol + 1)),
        pl.BlockSpec((tq, W), lambda b, r, i: (kv_prev(b, r, i), r * cpr + qcol + 2)),
        pl.BlockSpec((tq, W), lambda b, r, i: (b * nq + i, r * cpr + qcol + 2)),
    ]
    out_spec = pl.BlockSpec((tq, W), lambda b, r, i: (b * nq + i, r))
    o, lse = pl.pallas_call(
        functools.partial(_dil_attn_kernel, dil=dil, slopes=tuple(float(s) for s in slopes)),
        grid=(B, dil, nq),
        in_specs=in_specs,
        out_specs=[out_spec, out_spec],
        out_shape=[jax.ShapeDtypeStruct((T // dil, dil * W), BF16), jax.ShapeDtypeStruct((T // dil, dil * W), F32)],
        compiler_params=_params("parallel", "parallel", "arbitrary"),
        name=f"dil_attn_d{dil}",
    )(view, view, view, view, view)
    return o.reshape(T, W), lse.reshape(T, W)


def _sb_attn_kernel(q_ref, k_ref, v_ref, o_ref, c_sc, acc_sc, *, tq):
    i = pl.program_id(2)
    ii = lax.broadcasted_iota(jnp.int32, (tq, tq), 0)
    jj = lax.broadcasted_iota(jnp.int32, (tq, tq), 1)
    strict = ii > jj
    upper = jnp.where(strict, 1.0, 0.0).astype(BF16)
    q = q_ref[...] * jnp.asarray(QK_SCALE, BF16)
    lane = lax.broadcasted_iota(jnp.int32, q.shape, 1)
    zero = jnp.zeros_like(q)

    for hh in range(2):
        qh = jnp.where((lane >= HEAD_DIM) if hh else (lane < HEAD_DIM), q, zero)

        def tile(j, masked, hh=hh, qh=qh):
            r0 = pl.multiple_of(j * tq, tq)
            k = k_ref[pl.ds(r0, tq), :]
            v = v_ref[pl.ds(r0, tq), :]
            z = lax.dot_general(qh, k, _NT, preferred_element_type=F32)
            sp = jnp.maximum(z, 0.0) + jnp.log(1.0 + jnp.exp(-jnp.abs(z)))
            log_1m = -sp
            if masked:
                log_1m = jnp.where(strict, log_1m, 0.0)
            hi = log_1m.astype(BF16)
            lo = (log_1m - hi.astype(F32)).astype(BF16)
            rem = _dot(hi, upper) + _dot(lo, upper)
            c = c_sc[hh]
            a = jnp.exp(z - sp + rem + c)
            if masked:
                a = jnp.where(strict, a, 0.0)
            acc_sc[hh] = acc_sc[hh] + _dot(a.astype(BF16), v)
            c_sc[hh] = c + jnp.sum(log_1m, axis=1, keepdims=True)

        c_sc[hh] = jnp.zeros((tq, 1), F32)
        acc_sc[hh] = jnp.zeros((tq, LANES), F32)
        tile(i, True)

        def cond(j, hh=hh):
            return jnp.logical_and(j >= 0, jnp.max(c_sc[hh]) >= SB_SKIP_LOG)

        def body(j, tile=tile):
            tile(j, False)
            return j - 1

        lax.while_loop(cond, body, i - 1)

    o_ref[...] = jnp.where(lane < HEAD_DIM, acc_sc[0], acc_sc[1]).astype(o_ref.dtype)


def _sb_attn(proj, B, S, tq=256):
    T = B * S
    nq = S // tq
    W = SB_HEADS * HEAD_DIM
    nb = W // LANES
    base = 0
    return pl.pallas_call(
        functools.partial(_sb_attn_kernel, tq=tq),
        grid=(B, nb, nq),
        in_specs=[
            pl.BlockSpec((tq, LANES), lambda b, h, i: (b * nq + i, base + h)),
            pl.BlockSpec((S, LANES), lambda b, h, i: (b, base + nb + h)),
            pl.BlockSpec((S, LANES), lambda b, h, i: (b, base + 2 * nb + h)),
        ],
        out_specs=pl.BlockSpec((tq, LANES), lambda b, h, i: (b * nq + i, h)),
        out_shape=jax.ShapeDtypeStruct((T, W), BF16),
        scratch_shapes=[pltpu.VMEM((2, tq, 1), F32), pltpu.VMEM((2, tq, LANES), F32)],
        compiler_params=_params("parallel", "parallel", "arbitrary"),
        name="sb_attn",
    )(proj, proj, proj)


def _merge_kernel(x_ref, g_ref, ya_ref, o1_ref, o2_ref, o3_ref, l1_ref, l2_ref, l3_ref, yc_ref,
                  wg_ref, bg_ref, wb_ref, wo_ref, out_ref, *, n_chunk):
    x = x_ref[...]
    D = x.shape[1]
    h = _rms(x, g_ref[...]).astype(BF16)
    l1, l2, l3 = l1_ref[...], l2_ref[...], l3_ref[...]
    mx = jnp.maximum(jnp.maximum(l1, l2), l3)
    w1, w2, w3 = jnp.exp(l1 - mx), jnp.exp(l2 - mx), jnp.exp(l3 - mx)
    yb = ((w1 * o1_ref[...] + w2 * o2_ref[...] + w3 * o3_ref[...]) / (w1 + w2 + w3)).astype(BF16)
    ys = (ya_ref[...], yb, yc_ref[...])
    parts = []
    for c0 in range(0, D, n_chunk):
        m = jnp.zeros((x.shape[0], n_chunk), F32)
        for i in range(3):
            gc = slice(i * D + c0, i * D + c0 + n_chunk)
            gate = _sigmoid(_dot(h, wg_ref[:, gc]) + bg_ref[:, gc])
            m = m + gate * _dot(ys[i], wb_ref[i, :, c0:c0 + n_chunk])
        parts.append(m.astype(BF16))
    out_ref[...] = x + _dot(jnp.concatenate(parts, axis=1), wo_ref[...])


def _merge(x, g, ya, dil_outs, yc, w_gate, b_gate, w_branch, w_out, tm=256, n_chunk=512):
    T, D = x.shape
    W = ya.shape[1]
    row = lambda i: (i, 0)
    ysp = pl.BlockSpec((tm, W), row)
    (o1, s1), (o2, s2), (o3, s3) = dil_outs
    return pl.pallas_call(
        functools.partial(_merge_kernel, n_chunk=n_chunk),
        grid=(T // tm,),
        in_specs=[pl.BlockSpec((tm, D), row), _const_spec((1, D)), ysp, ysp, ysp, ysp, ysp, ysp, ysp, ysp,
                  _const_spec(w_gate.shape), _const_spec((1, 3 * D)), _const_spec(w_branch.shape),
                  _const_spec(w_out.shape)],
        out_specs=pl.BlockSpec((tm, D), row),
        out_shape=jax.ShapeDtypeStruct((T, D), F32),
        compiler_params=_params("parallel"),
        name="merge",
    )(x, g.reshape(1, D), ya, o1, o2, o3, s1, s2, s3, yc, w_gate, b_gate.reshape(1, -1), w_branch, w_out)


def _ffn_dense_kernel(x_ref, g_ref, wg_ref, wu_ref, wd_ref, gf_ref, out_ref, *, f_chunk, final):
    x = x_ref[...]
    h = _rms(x, g_ref[...]).astype(BF16)
    acc = x
    for f0 in range(0, wg_ref.shape[1], f_chunk):
        a = _dot(h, wg_ref[:, f0:f0 + f_chunk])
        u = _dot(h, wu_ref[:, f0:f0 + f_chunk])
        acc = acc + _dot((a * _sigmoid(a) * u).astype(BF16), wd_ref[f0:f0 + f_chunk, :])
    out_ref[...] = _rms(acc, gf_ref[...]) if final else acc


def _ffn_dense(x, g, w_g, w_u, w_d, g_final, final, tm=512, f_chunk=256):
    T, D = x.shape
    row = lambda i: (i, 0)
    return pl.pallas_call(
        functools.partial(_ffn_dense_kernel, f_chunk=f_chunk, final=final),
        grid=(T // tm,),
        in_specs=[pl.BlockSpec((tm, D), row), _const_spec((1, D)), _const_spec(w_g.shape), _const_spec(w_u.shape),
                  _const_spec(w_d.shape), _const_spec((1, D))],
        out_specs=pl.BlockSpec((tm, D), row),
        out_shape=jax.ShapeDtypeStruct((T, D), F32),
        compiler_params=_params("parallel"),
        name="ffn_dense",
    )(x, g.reshape(1, D), w_g, w_u, w_d, g_final.reshape(1, D))


def _moe_kernel(x_ref, g_ref, wr_ref, wg_ref, wu_ref, wd_ref, gf_ref, out_ref, h_sc, gate_sc, acc_sc, *, final):
    e = pl.program_id(1)
    f = pl.program_id(2)
    lane = lax.broadcasted_iota(jnp.int32, gate_sc.shape, 1)

    @pl.when(jnp.logical_and(e == 0, f == 0))
    def _():
        x = x_ref[...]
        hf = _rms(x, g_ref[...])
        h_sc[...] = hf.astype(BF16)
        logits = jnp.dot(hf, wr_ref[...], preferred_element_type=F32, precision=lax.Precision.HIGHEST)
        logits = jnp.where(lane < N_EXPERTS, logits, NEG_INF)
        v1 = jnp.max(logits, axis=1, keepdims=True)
        i1 = jnp.min(jnp.where(logits == v1, lane, LANES), axis=1, keepdims=True)
        rest = jnp.where(lane == i1, NEG_INF, logits)
        v2 = jnp.max(rest, axis=1, keepdims=True)
        i2 = jnp.min(jnp.where(rest == v2, lane, LANES), axis=1, keepdims=True)
        e2 = jnp.exp(v2 - v1)
        gate_sc[...] = jnp.where(lane == i1, 1.0 / (1.0 + e2), jnp.where(lane == i2, e2 / (1.0 + e2), 0.0))
        acc_sc[...] = x

    h = h_sc[...]
    a = _dot(h, wg_ref[0])
    u = _dot(h, wu_ref[0])
    ge = jnp.sum(jnp.where(lane == e, gate_sc[...], 0.0), axis=1, keepdims=True)
    acc_sc[...] += _dot((a * _sigmoid(a) * u * ge).astype(BF16), wd_ref[0])

    @pl.when(jnp.logical_and(e == pl.num_programs(1) - 1, f == pl.num_programs(2) - 1))
    def _():
        acc = acc_sc[...]
        out_ref[...] = _rms(acc, gf_ref[...]) if final else acc


def _moe(x, g, w_router, w_g, w_u, w_d, g_final, final, tm=512, tf=1792):
    T, D = x.shape
    E, _, F = w_g.shape
    wr = jnp.zeros((D, LANES), F32).at[:, :E].set(w_router)
    row = lambda i, e, f: (i, 0)
    return pl.pallas_call(
        functools.partial(_moe_kernel, final=final),
        grid=(T // tm, E, F // tf),
        in_specs=[pl.BlockSpec((tm, D), row), _const_spec((1, D)), _const_spec((D, LANES)),
                  pl.BlockSpec((1, D, tf), lambda i, e, f: (e, 0, f)),
                  pl.BlockSpec((1, D, tf), lambda i, e, f: (e, 0, f)),
                  pl.BlockSpec((1, tf, D), lambda i, e, f: (e, f, 0)),
                  _const_spec((1, D))],
        out_specs=pl.BlockSpec((tm, D), row),
        out_shape=jax.ShapeDtypeStruct((T, D), F32),
        scratch_shapes=[pltpu.VMEM((tm, D), BF16), pltpu.VMEM((tm, LANES), F32), pltpu.VMEM((tm, D), F32)],
        compiler_params=_params("parallel", "arbitrary", "arbitrary"),
        name="moe",
    )(x, g.reshape(1, D), wr, w_g, w_u, w_d, g_final.reshape(1, D))


def kernel(x, norm_mix, w_in, diff_lambda, diff_subln, w_gate, b_gate, w_branch, w_out, norm_ffn, w_dense_gate,
           w_dense_up, w_dense_down, w_router, w_moe_gate, w_moe_up, w_moe_down, norm_final):
    B, S, D = x.shape
    depth = norm_mix.shape[0]
    in_width = w_in.shape[2]
    slopes_diff, slopes_dil = _alibi_slopes()
    nslopes_diff = jnp.asarray(-slopes_diff, F32)
    xf = x.reshape(B * S, D)
    for layer in range(depth):
        pa, pb, pc = _norm_proj(xf, norm_mix[layer], w_in[layer].astype(BF16), 3)
        lam_init = 0.8 - 0.6 * math.exp(-0.3 * layer)
        ya = _diff_attn(pa, diff_lambda[layer], diff_subln[layer], nslopes_diff, B, S, lam_init)
        dil_outs = [_dil_attn(pb, slopes_dil, B, S, dil, in_width // 3) for _, dil in DIL_PATTERNS]
        yc = _sb_attn(pc, B, S)
        xf = _merge(xf, norm_mix[layer], ya, dil_outs, yc, w_gate[layer].astype(BF16), b_gate[layer],
                    w_branch[layer].astype(BF16), w_out[layer].astype(BF16))
        final = layer == depth - 1
        j = layer // 2
        if layer % 2 == 0:
            xf = _ffn_dense(xf, norm_ffn[layer], w_dense_gate[j].astype(BF16), w_dense_up[j].astype(BF16),
                            w_dense_down[j].astype(BF16), norm_final, final)
        else:
            xf = _moe(xf, norm_ffn[layer], w_router[j], w_moe_gate[j].astype(BF16), w_moe_up[j].astype(BF16),
                      w_moe_down[j].astype(BF16), norm_final, final)
    return xf.reshape(B, S, D)
```

```python
import functools
import math

import numpy as np
import jax
import jax.numpy as jnp
from jax import lax
from jax.experimental import pallas as pl
from jax.experimental.pallas import tpu as pltpu

F32 = jnp.float32
BF16 = jnp.bfloat16

HEAD_DIM = 64
QK_SCALE = HEAD_DIM ** -0.5
DIFF_HEADS = 4
DIL_HEADS = 8
SB_HEADS = 8
DIL_PATTERNS = ((128, 1), (512, 4), (2048, 16))
DIL_SPAN = 128
N_EXPERTS = 8
NORM_EPS = 1e-6
NEG_INF = -1e30
SB_SKIP_LOG = -104.0
LANES = 128
VMEM_LIMIT = 56 * 1024 * 1024

_NT = (((1,), (1,)), ((), ()))


def _alibi_slopes():
    n = DIFF_HEADS + DIL_HEADS
    slopes = (2.0 ** (-8.0 * np.arange(1, n + 1) / n)).astype(np.float32)
    is_diff = (np.arange(n) % 3) == 2
    return slopes[is_diff], slopes[~is_diff]


def _rms(x, g):
    ms = jnp.mean(x * x, axis=-1, keepdims=True)
    return x * lax.rsqrt(ms + NORM_EPS) * g


def _dot(a, b):
    return jnp.dot(a, b, preferred_element_type=F32)


def _sigmoid(a):
    return 1.0 / (1.0 + jnp.exp(-a))


def _params(*sem):
    return pltpu.CompilerParams(dimension_semantics=sem, vmem_limit_bytes=VMEM_LIMIT)


def _const_spec(shape):
    nd = len(shape)
    return pl.BlockSpec(shape, lambda *_: (0,) * nd)


def _norm_proj_kernel(x_ref, g_ref, w_ref, *o_refs, n_chunk):
    h = _rms(x_ref[...], g_ref[...]).astype(BF16)
    col = 0
    for o_ref in o_refs:
        for n0 in range(0, o_ref.shape[1], n_chunk):
            o_ref[:, n0:n0 + n_chunk] = _dot(h, w_ref[:, col + n0:col + n0 + n_chunk]).astype(o_ref.dtype)
        col += o_ref.shape[1]


def _norm_proj(x, g, w, n_out, tm=512, n_chunk=512):
    T, D = x.shape
    N = w.shape[1] // n_out
    return pl.pallas_call(
        functools.partial(_norm_proj_kernel, n_chunk=n_chunk),
        grid=(T // tm,),
        in_specs=[pl.BlockSpec((tm, D), lambda i: (i, 0)), _const_spec((1, D)), _const_spec(w.shape)],
        out_specs=[pl.BlockSpec((tm, N), lambda i: (i, 0))] * n_out,
        out_shape=[jax.ShapeDtypeStruct((T, N), BF16)] * n_out,
        compiler_params=_params("parallel"),
        name="norm_proj",
    )(x, g.reshape(1, D), w)


def _diff_attn_kernel(nslope_ref, q_ref, k_ref, v_ref, lam_ref, subln_ref, o_ref, m_sc, acc_sc, *, tq, lam_init):
    i = pl.program_id(2)
    nslope = nslope_ref[pl.program_id(1)]
    ii = lax.broadcasted_iota(jnp.int32, (tq, tq), 0)
    jj = lax.broadcasted_iota(jnp.int32, (tq, tq), 1)
    rel = (ii - jj).astype(F32) * nslope
    causal = ii >= jj
    q = q_ref[...] * jnp.asarray(QK_SCALE, BF16)
    lane = lax.broadcasted_iota(jnp.int32, q.shape, 1)
    zero = jnp.zeros_like(q)
    qs = (jnp.where(lane < HEAD_DIM, q, zero), jnp.where(lane >= HEAD_DIM, q, zero))
    m_sc[...] = jnp.full(m_sc.shape, NEG_INF, F32)
    acc_sc[...] = jnp.zeros(acc_sc.shape, F32)

    def tile(j, masked):
        r0 = pl.multiple_of(j * tq, tq)
        k = k_ref[pl.ds(r0, tq), :]
        v = v_ref[pl.ds(r0, tq), :]
        v1 = jnp.concatenate([v, jnp.ones_like(v)], axis=1)
        off = jnp.full((tq, 1), (i - j) * tq, jnp.int32).astype(F32) * nslope
        for c in range(2):
            s = lax.dot_general(qs[c], k, _NT, preferred_element_type=F32)
            t = s + rel
            if masked:
                t = jnp.where(causal, t, NEG_INF)
            m_prev = m_sc[c]
            m_new = jnp.maximum(m_prev, jnp.max(t, axis=1, keepdims=True) + off)
            alpha = jnp.exp(m_prev - m_new)
            p = jnp.exp(t - (m_new - off))
            acc_sc[c] = alpha * acc_sc[c] + _dot(p.astype(BF16), v1)
            m_sc[c] = m_new

    def body(j, carry):
        tile(j, False)
        return carry

    lax.fori_loop(0, i, body, 0)
    tile(i, True)

    lv = lam_ref[...]
    lam = (jnp.exp(jnp.sum(lv[0:1] * lv[1:2], axis=1, keepdims=True))
           - jnp.exp(jnp.sum(lv[2:3] * lv[3:4], axis=1, keepdims=True)) + lam_init)
    a0, a1 = acc_sc[0], acc_sc[1]
    o = a0[:, :LANES] / a0[:, LANES:] - lam * (a1[:, :LANES] / a1[:, LANES:])
    o_ref[...] = (_rms(o, subln_ref[...]) * (1.0 - lam_init)).astype(o_ref.dtype)


def _diff_attn(proj, lam_vec, subln, nslopes, B, S, lam_init, tq=512):
    T = B * S
    nq = S // tq
    W = DIFF_HEADS * 2 * HEAD_DIM
    nh = DIFF_HEADS
    return pl.pallas_call(
        functools.partial(_diff_attn_kernel, tq=tq, lam_init=lam_init),
        grid=(B, nh, nq),
        in_specs=[
            pl.BlockSpec(memory_space=pltpu.SMEM),
            pl.BlockSpec((tq, LANES), lambda b, h, i: (b * nq + i, h)),
            pl.BlockSpec((S, LANES), lambda b, h, i: (b, nh + h)),
            pl.BlockSpec((S, LANES), lambda b, h, i: (b, 2 * nh + h)),
            _const_spec((4, HEAD_DIM)),
            _const_spec((1, 2 * HEAD_DIM)),
        ],
        out_specs=pl.BlockSpec((tq, LANES), lambda b, h, i: (b * nq + i, h)),
        out_shape=jax.ShapeDtypeStruct((T, W), BF16),
        scratch_shapes=[pltpu.VMEM((2, tq, 1), F32), pltpu.VMEM((2, tq, 2 * LANES), F32)],
        compiler_params=_params("parallel", "parallel", "arbitrary"),
        name="diff_attn",
    )(nslopes, proj, proj, proj, lam_vec, subln.reshape(1, -1))


def _dil_attn_kernel(q_ref, kp_ref, kc_ref, vp_ref, vc_ref, o_ref, lse_ref, *, dil, slopes):
    i = pl.program_id(2)
    tq = DIL_SPAN
    qi = lax.broadcasted_iota(jnp.int32, (tq, 2 * tq), 0)
    kj = lax.broadcasted_iota(jnp.int32, (tq, 2 * tq), 1)
    delta = qi - kj + DIL_SPAN
    first_key = jnp.where(i > 0, 0, tq)
    valid = (delta >= 0) & (delta <= DIL_SPAN) & (kj >= first_key)
    deltaf = delta.astype(F32)
    lane = lax.broadcasted_iota(jnp.int32, (tq, LANES), 1)
    low = lane < HEAD_DIM
    ones = jnp.ones((2 * tq, LANES), BF16)
    for hp in range(DIL_HEADS // 2):
        cs = slice(hp * LANES, (hp + 1) * LANES)
        q = q_ref[:, cs] * jnp.asarray(QK_SCALE, BF16)
        zero = jnp.zeros_like(q)
        k = jnp.concatenate([kp_ref[:, cs], kc_ref[:, cs]], axis=0)
        v1 = jnp.concatenate([jnp.concatenate([vp_ref[:, cs], vc_ref[:, cs]], axis=0), ones], axis=1)
        res, lses = [], []
        for hh in range(2):
            qh = jnp.where(jnp.logical_not(low) if hh else low, q, zero)
            s = lax.dot_general(qh, k, _NT, preferred_element_type=F32)
            s = jnp.where(valid, s + deltaf * float(-slopes[2 * hp + hh] * dil), NEG_INF)
            m = jnp.max(s, axis=1, keepdims=True)
            r = _dot(jnp.exp(s - m).astype(BF16), v1)
            res.append(r[:, :LANES] / r[:, LANES:])
            lses.append(m + jnp.log(r[:, LANES:]))
        o_ref[:, cs] = jnp.where(low, res[0], res[1]).astype(o_ref.dtype)
        lse_ref[:, cs] = jnp.where(low, lses[0], lses[1])


def _dil_attn(proj, slopes, B, S, dil, in_width):
    T = B * S
    tq = DIL_SPAN
    n = S // dil
    nq = n // tq
    W = DIL_HEADS * HEAD_DIM
    cpr = in_width // W
    qcol = 0
    view = proj.reshape(T // dil, dil * in_width)
    kv_prev = lambda b, r, i: b * nq + jnp.maximum(i - 1, 0)
    in_specs = [
        pl.BlockSpec((tq, W), lambda b, r, i: (b * nq + i, r * cpr + qcol)),
        pl.BlockSpec((tq, W), lambda b, r, i: (kv_prev(b, r, i), r * cpr + qcol + 1)),
        pl.BlockSpec((tq, W), lambda b, r, i: (b * nq + i, r * cpr + qcol + 1)),
        pl.BlockSpec((tq, W), lambda b, r, i: (kv_prev(b, r, i), r * cpr + qcol + 2)),
        pl.BlockSpec((tq, W), lambda b, r, i: (b * nq + i, r * cpr + qcol + 2)),
    ]
    out_spec = pl.BlockSpec((tq, W), lambda b, r, i: (b * nq + i, r))
    o, lse = pl.pallas_call(
        functools.partial(_dil_attn_kernel, dil=dil, slopes=tuple(float(s) for s in slopes)),
        grid=(B, dil, nq),
        in_specs=in_specs,
        out_specs=[out_spec, out_spec],
        out_shape=[jax.ShapeDtypeStruct((T // dil, dil * W), BF16), jax.ShapeDtypeStruct((T // dil, dil * W), F32)],
        compiler_params=_params("parallel", "parallel", "arbitrary"),
        name=f"dil_attn_d{dil}",
    )(view, view, view, view, view)
    return o.reshape(T, W), lse.reshape(T, W)


def _sb_attn_kernel(q_ref, k_ref, v_ref, o_ref, c_sc, acc_sc, *, tq):
    i = pl.program_id(2)
    ii = lax.broadcasted_iota(jnp.int32, (tq, tq), 0)
    jj = lax.broadcasted_iota(jnp.int32, (tq, tq), 1)
    strict = ii > jj
    upper = jnp.where(strict, 1.0, 0.0).astype(BF16)
    q = q_ref[...] * jnp.asarray(QK_SCALE, BF16)
    lane = lax.broadcasted_iota(jnp.int32, q.shape, 1)
    zero = jnp.zeros_like(q)

    qhs = (jnp.where(lane < HEAD_DIM, q, zero), jnp.where(lane >= HEAD_DIM, q, zero))

    def tile(j, masked):
        r0 = pl.multiple_of(j * tq, tq)
        k = k_ref[pl.ds(r0, tq), :]
        v = v_ref[pl.ds(r0, tq), :]
        for hh in range(2):
            z = lax.dot_general(qhs[hh], k, _NT, preferred_element_type=F32)
            sp = jnp.maximum(z, 0.0) + jnp.log(1.0 + jnp.exp(-jnp.abs(z)))
            log_1m = -sp
            if masked:
                log_1m = jnp.where(strict, log_1m, 0.0)
            hi = log_1m.astype(BF16)
            lo = (log_1m - hi.astype(F32)).astype(BF16)
            rem = _dot(hi, upper) + _dot(lo, upper)
            c = c_sc[hh]
            a = jnp.exp(z - sp + rem + c)
            if masked:
                a = jnp.where(strict, a, 0.0)
            acc_sc[hh] = acc_sc[hh] + _dot(a.astype(BF16), v)
            c_sc[hh] = c + jnp.sum(log_1m, axis=1, keepdims=True)

    c_sc[...] = jnp.zeros(c_sc.shape, F32)
    acc_sc[...] = jnp.zeros(acc_sc.shape, F32)
    tile(i, True)

    def cond(j):
        return jnp.logical_and(j >= 0, jnp.max(jnp.maximum(c_sc[0], c_sc[1])) >= SB_SKIP_LOG)

    def body(j):
        tile(j, False)
        return j - 1

    lax.while_loop(cond, body, i - 1)

    o_ref[...] = jnp.where(lane < HEAD_DIM, acc_sc[0], acc_sc[1]).astype(o_ref.dtype)


def _sb_attn(proj, B, S, tq=256):
    T = B * S
    nq = S // tq
    W = SB_HEADS * HEAD_DIM
    nb = W // LANES
    base = 0
    return pl.pallas_call(
        functools.partial(_sb_attn_kernel, tq=tq),
        grid=(B, nb, nq),
        in_specs=[
            pl.BlockSpec((tq, LANES), lambda b, h, i: (b * nq + i, base + h)),
            pl.BlockSpec((S, LANES), lambda b, h, i: (b, base + nb + h)),
            pl.BlockSpec((S, LANES), lambda b, h, i: (b, base + 2 * nb + h)),
        ],
        out_specs=pl.BlockSpec((tq, LANES), lambda b, h, i: (b * nq + i, h)),
        out_shape=jax.ShapeDtypeStruct((T, W), BF16),
        scratch_shapes=[pltpu.VMEM((2, tq, 1), F32), pltpu.VMEM((2, tq, LANES), F32)],
        compiler_params=_params("parallel", "parallel", "arbitrary"),
        name="sb_attn",
    )(proj, proj, proj)


def _merge_kernel(x_ref, g_ref, ya_ref, o1_ref, o2_ref, o3_ref, l1_ref, l2_ref, l3_ref, yc_ref,
                  wg_ref, bg_ref, wb_ref, wo_ref, out_ref, *, n_chunk):
    x = x_ref[...]
    D = x.shape[1]
    h = _rms(x, g_ref[...]).astype(BF16)
    l1, l2, l3 = l1_ref[...], l2_ref[...], l3_ref[...]
    mx = jnp.maximum(jnp.maximum(l1, l2), l3)
    w1, w2, w3 = jnp.exp(l1 - mx), jnp.exp(l2 - mx), jnp.exp(l3 - mx)
    yb = ((w1 * o1_ref[...] + w2 * o2_ref[...] + w3 * o3_ref[...]) / (w1 + w2 + w3)).astype(BF16)
    ys = (ya_ref[...], yb, yc_ref[...])
    parts = []
    for c0 in range(0, D, n_chunk):
        m = jnp.zeros((x.shape[0], n_chunk), F32)
        for i in range(3):
            gc = slice(i * D + c0, i * D + c0 + n_chunk)
            gate = _sigmoid(_dot(h, wg_ref[:, gc]) + bg_ref[:, gc])
            m = m + gate * _dot(ys[i], wb_ref[i, :, c0:c0 + n_chunk])
        parts.append(m.astype(BF16))
    out_ref[...] = x + _dot(jnp.concatenate(parts, axis=1), wo_ref[...])


def _merge(x, g, ya, dil_outs, yc, w_gate, b_gate, w_branch, w_out, tm=256, n_chunk=512):
    T, D = x.shape
    W = ya.shape[1]
    row = lambda i: (i, 0)
    ysp = pl.BlockSpec((tm, W), row)
    (o1, s1), (o2, s2), (o3, s3) = dil_outs
    return pl.pallas_call(
        functools.partial(_merge_kernel, n_chunk=n_chunk),
        grid=(T // tm,),
        in_specs=[pl.BlockSpec((tm, D), row), _const_spec((1, D)), ysp, ysp, ysp, ysp, ysp, ysp, ysp, ysp,
                  _const_spec(w_gate.shape), _const_spec((1, 3 * D)), _const_spec(w_branch.shape),
                  _const_spec(w_out.shape)],
        out_specs=pl.BlockSpec((tm, D), row),
        out_shape=jax.ShapeDtypeStruct((T, D), F32),
        compiler_params=_params("parallel"),
        name="merge",
    )(x, g.reshape(1, D), ya, o1, o2, o3, s1, s2, s3, yc, w_gate, b_gate.reshape(1, -1), w_branch, w_out)


def _ffn_dense_kernel(x_ref, g_ref, wg_ref, wu_ref, wd_ref, gf_ref, out_ref, *, f_chunk, final):
    x = x_ref[...]
    h = _rms(x, g_ref[...]).astype(BF16)
    acc = x
    for f0 in range(0, wg_ref.shape[1], f_chunk):
        a = _dot(h, wg_ref[:, f0:f0 + f_chunk])
        u = _dot(h, wu_ref[:, f0:f0 + f_chunk])
        acc = acc + _dot((a * _sigmoid(a) * u).astype(BF16), wd_ref[f0:f0 + f_chunk, :])
    out_ref[...] = _rms(acc, gf_ref[...]) if final else acc


def _ffn_dense(x, g, w_g, w_u, w_d, g_final, final, tm=512, f_chunk=256):
    T, D = x.shape
    row = lambda i: (i, 0)
    return pl.pallas_call(
        functools.partial(_ffn_dense_kernel, f_chunk=f_chunk, final=final),
        grid=(T // tm,),
        in_specs=[pl.BlockSpec((tm, D), row), _const_spec((1, D)), _const_spec(w_g.shape), _const_spec(w_u.shape),
                  _const_spec(w_d.shape), _const_spec((1, D))],
        out_specs=pl.BlockSpec((tm, D), row),
        out_shape=jax.ShapeDtypeStruct((T, D), F32),
        compiler_params=_params("parallel"),
        name="ffn_dense",
    )(x, g.reshape(1, D), w_g, w_u, w_d, g_final.reshape(1, D))


def _moe_kernel(x_ref, g_ref, wr_ref, wg_ref, wu_ref, wd_ref, gf_ref, out_ref, h_sc, gate_sc, acc_sc, *, final):
    e = pl.program_id(1)
    f = pl.program_id(2)
    lane = lax.broadcasted_iota(jnp.int32, gate_sc.shape, 1)

    @pl.when(jnp.logical_and(e == 0, f == 0))
    def _():
        x = x_ref[...]
        hf = _rms(x, g_ref[...])
        h_sc[...] = hf.astype(BF16)
        logits = jnp.dot(hf, wr_ref[...], preferred_element_type=F32, precision=lax.Precision.HIGHEST)
        logits = jnp.where(lane < N_EXPERTS, logits, NEG_INF)
        v1 = jnp.max(logits, axis=1, keepdims=True)
        i1 = jnp.min(jnp.where(logits == v1, lane, LANES), axis=1, keepdims=True)
        rest = jnp.where(lane == i1, NEG_INF, logits)
        v2 = jnp.max(rest, axis=1, keepdims=True)
        i2 = jnp.min(jnp.where(rest == v2, lane, LANES), axis=1, keepdims=True)
        e2 = jnp.exp(v2 - v1)
        gate_sc[...] = jnp.where(lane == i1, 1.0 / (1.0 + e2), jnp.where(lane == i2, e2 / (1.0 + e2), 0.0))
        acc_sc[...] = x

    h = h_sc[...]
    a = _dot(h, wg_ref[0])
    u = _dot(h, wu_ref[0])
    ge = jnp.sum(jnp.where(lane == e, gate_sc[...], 0.0), axis=1, keepdims=True)
    acc_sc[...] += _dot((a * _sigmoid(a) * u * ge).astype(BF16), wd_ref[0])

    @pl.when(jnp.logical_and(e == pl.num_programs(1) - 1, f == pl.num_programs(2) - 1))
    def _():
        acc = acc_sc[...]
        out_ref[...] = _rms(acc, gf_ref[...]) if final else acc


def _moe(x, g, w_router, w_g, w_u, w_d, g_final, final, tm=512, tf=1792):
    T, D = x.shape
    E, _, F = w_g.shape
    wr = jnp.zeros((D, LANES), F32).at[:, :E].set(w_router)
    row = lambda i, e, f: (i, 0)
    return pl.pallas_call(
        functools.partial(_moe_kernel, final=final),
        grid=(T // tm, E, F // tf),
        in_specs=[pl.BlockSpec((tm, D), row), _const_spec((1, D)), _const_spec((D, LANES)),
                  pl.BlockSpec((1, D, tf), lambda i, e, f: (e, 0, f)),
                  pl.BlockSpec((1, D, tf), lambda i, e, f: (e, 0, f)),
                  pl.BlockSpec((1, tf, D), lambda i, e, f: (e, f, 0)),
                  _const_spec((1, D))],
        out_specs=pl.BlockSpec((tm, D), row),
        out_shape=jax.ShapeDtypeStruct((T, D), F32),
        scratch_shapes=[pltpu.VMEM((tm, D), BF16), pltpu.VMEM((tm, LANES), F32), pltpu.VMEM((tm, D), F32)],
        compiler_params=_params("parallel", "arbitrary", "arbitrary"),
        name="moe",
    )(x, g.reshape(1, D), wr, w_g, w_u, w_d, g_final.reshape(1, D))


def kernel(x, norm_mix, w_in, diff_lambda, diff_subln, w_gate, b_gate, w_branch, w_out, norm_ffn, w_dense_gate,
           w_dense_up, w_dense_down, w_router, w_moe_gate, w_moe_up, w_moe_down, norm_final):
    B, S, D = x.shape
    depth = norm_mix.shape[0]
    in_width = w_in.shape[2]
    slopes_diff, slopes_dil = _alibi_slopes()
    nslopes_diff = jnp.asarray(-slopes_diff, F32)
    xf = x.reshape(B * S, D)
    for layer in range(depth):
        pa, pb, pc = _norm_proj(xf, norm_mix[layer], w_in[layer].astype(BF16), 3)
        lam_init = 0.8 - 0.6 * math.exp(-0.3 * layer)
        ya = _diff_attn(pa, diff_lambda[layer], diff_subln[layer], nslopes_diff, B, S, lam_init)
        dil_outs = [_dil_attn(pb, slopes_dil, B, S, dil, in_width // 3) for _, dil in DIL_PATTERNS]
        yc = _sb_attn(pc, B, S)
        xf = _merge(xf, norm_mix[layer], ya, dil_outs, yc, w_gate[layer].astype(BF16), b_gate[layer],
                    w_branch[layer].astype(BF16), w_out[layer].astype(BF16))
        final = layer == depth - 1
        j = layer // 2
        if layer % 2 == 0:
            xf = _ffn_dense(xf, norm_ffn[layer], w_dense_gate[j].astype(BF16), w_dense_up[j].astype(BF16),
                            w_dense_down[j].astype(BF16), norm_final, final)
        else:
            xf = _moe(xf, norm_ffn[layer], w_router[j], w_moe_gate[j].astype(BF16), w_moe_up[j].astype(BF16),
                      w_moe_down[j].astype(BF16), norm_final, final)
    return xf.reshape(B, S, D)
```

```python
import functools
import math

import numpy as np
import jax
import jax.numpy as jnp
from jax import lax
from jax.experimental import pallas as pl
from jax.experimental.pallas import tpu as pltpu

F32 = jnp.float32
BF16 = jnp.bfloat16

HEAD_DIM = 64
QK_SCALE = HEAD_DIM ** -0.5
DIFF_HEADS = 4
DIL_HEADS = 8
SB_HEADS = 8
DIL_PATTERNS = ((128, 1), (512, 4), (2048, 16))
DIL_SPAN = 128
N_EXPERTS = 8
NORM_EPS = 1e-6
NEG_INF = -1e30
SB_SKIP_LOG = -104.0
LANES = 128
VMEM_LIMIT = 56 * 1024 * 1024

_NT = (((1,), (1,)), ((), ()))


def _alibi_slopes():
    n = DIFF_HEADS + DIL_HEADS
    slopes = (2.0 ** (-8.0 * np.arange(1, n + 1) / n)).astype(np.float32)
    is_diff = (np.arange(n) % 3) == 2
    return slopes[is_diff], slopes[~is_diff]


def _rms(x, g):
    ms = jnp.mean(x * x, axis=-1, keepdims=True)
    return x * lax.rsqrt(ms + NORM_EPS) * g


def _dot(a, b):
    return jnp.dot(a, b, preferred_element_type=F32)


def _sigmoid(a):
    return 1.0 / (1.0 + jnp.exp(-a))


def _params(*sem):
    return pltpu.CompilerParams(dimension_semantics=sem, vmem_limit_bytes=VMEM_LIMIT)


def _const_spec(shape):
    nd = len(shape)
    return pl.BlockSpec(shape, lambda *_: (0,) * nd)


def _norm_proj_kernel(x_ref, g_ref, w_ref, pa_ref, pb_ref, pc_ref, *rest, n_chunk, dils):
    pbd_refs, res_sc = rest[:-1], rest[-1]
    h = _rms(x_ref[...], g_ref[...]).astype(BF16)
    tm = h.shape[0]
    N = pa_ref.shape[1]
    for n0 in range(0, N, n_chunk):
        pa_ref[:, n0:n0 + n_chunk] = _dot(h, w_ref[:, n0:n0 + n_chunk]).astype(pa_ref.dtype)
        pc_ref[:, n0:n0 + n_chunk] = _dot(h, w_ref[:, 2 * N + n0:2 * N + n0 + n_chunk]).astype(pc_ref.dtype)
        res = _dot(h, w_ref[:, N + n0:N + n0 + n_chunk])
        pb_ref[:, n0:n0 + n_chunk] = res.astype(pb_ref.dtype)
        for g in range(n_chunk // LANES):
            res_sc[g] = res[:, g * LANES:(g + 1) * LANES]
        for d, o_ref in zip(dils, pbd_refs):
            for r in range(d):
                for g in range(n_chunk // LANES):
                    c0 = r * N + n0 + g * LANES
                    o_ref[:, c0:c0 + LANES] = res_sc[g, pl.ds(r, tm // d, stride=d), :].astype(o_ref.dtype)


def _norm_proj(x, g, w, dils, tm=512, n_chunk=512):
    T, D = x.shape
    N = w.shape[1] // 3
    row = lambda i: (i, 0)
    return pl.pallas_call(
        functools.partial(_norm_proj_kernel, n_chunk=n_chunk, dils=dils),
        grid=(T // tm,),
        in_specs=[pl.BlockSpec((tm, D), row), _const_spec((1, D)), _const_spec(w.shape)],
        out_specs=[pl.BlockSpec((tm, N), row)] * 3 + [pl.BlockSpec((tm // d, d * N), row) for d in dils],
        out_shape=[jax.ShapeDtypeStruct((T, N), BF16)] * 3
                  + [jax.ShapeDtypeStruct((T // d, d * N), BF16) for d in dils],
        scratch_shapes=[pltpu.VMEM((n_chunk // LANES, tm, LANES), F32)],
        compiler_params=_params("parallel"),
        name="norm_proj",
    )(x, g.reshape(1, D), w)


def _diff_attn_kernel(nslope_ref, q_ref, k_ref, v_ref, lam_ref, subln_ref, o_ref, m_sc, acc_sc, *, tq, lam_init):
    i = pl.program_id(2)
    hb = q_ref.shape[1] // LANES
    nslopes = [nslope_ref[pl.program_id(1) * hb + g] for g in range(hb)]
    ii = lax.broadcasted_iota(jnp.int32, (tq, tq), 0)
    jj = lax.broadcasted_iota(jnp.int32, (tq, tq), 1)
    dist = (ii - jj).astype(F32)
    rels = [dist * ns for ns in nslopes]
    causal = ii >= jj
    q = q_ref[...] * jnp.asarray(QK_SCALE, BF16)
    half_of_lane = lax.broadcasted_iota(jnp.int32, q.shape, 1) // HEAD_DIM
    zero = jnp.zeros_like(q)
    qs = [jnp.where(half_of_lane == n, q, zero) for n in range(2 * hb)]
    m_sc[...] = jnp.full(m_sc.shape, NEG_INF, F32)
    acc_sc[...] = jnp.zeros(acc_sc.shape, F32)

    def tile(j, masked):
        r0 = pl.multiple_of(j * tq, tq)
        k = k_ref[pl.ds(r0, tq), :]
        v = v_ref[pl.ds(r0, tq), :]
        ones = jnp.ones((tq, LANES), BF16)
        for g in range(hb):
            v1 = jnp.concatenate([v[:, g * LANES:(g + 1) * LANES], ones], axis=1)
            off = jnp.full((tq, 1), (i - j) * tq, jnp.int32).astype(F32) * nslopes[g]
            for c in range(2):
                n = 2 * g + c
                s = lax.dot_general(qs[n], k, _NT, preferred_element_type=F32)
                t = s + rels[g]
                if masked:
                    t = jnp.where(causal, t, NEG_INF)
                m_prev = m_sc[n]
                m_new = jnp.maximum(m_prev, jnp.max(t, axis=1, keepdims=True) + off)
                alpha = jnp.exp(m_prev - m_new)
                p = jnp.exp(t - (m_new - off))
                acc_sc[n] = alpha * acc_sc[n] + _dot(p.astype(BF16), v1)
                m_sc[n] = m_new

    def body(j, carry):
        tile(j, False)
        return carry

    lax.fori_loop(0, i, body, 0)
    tile(i, True)

    lv = lam_ref[...]
    lam = (jnp.exp(jnp.sum(lv[0:1] * lv[1:2], axis=1, keepdims=True))
           - jnp.exp(jnp.sum(lv[2:3] * lv[3:4], axis=1, keepdims=True)) + lam_init)
    for g in range(hb):
        a0, a1 = acc_sc[2 * g], acc_sc[2 * g + 1]
        o = a0[:, :LANES] / a0[:, LANES:] - lam * (a1[:, :LANES] / a1[:, LANES:])
        o_ref[:, g * LANES:(g + 1) * LANES] = (_rms(o, subln_ref[...]) * (1.0 - lam_init)).astype(o_ref.dtype)


def _diff_attn(proj, lam_vec, subln, nslopes, B, S, lam_init, tq=512, hb=1):
    T = B * S
    nq = S // tq
    W = DIFF_HEADS * 2 * HEAD_DIM
    nb = DIFF_HEADS // hb
    bw = hb * LANES
    return pl.pallas_call(
        functools.partial(_diff_attn_kernel, tq=tq, lam_init=lam_init),
        grid=(B, nb, nq),
        in_specs=[
            pl.BlockSpec(memory_space=pltpu.SMEM),
            pl.BlockSpec((tq, bw), lambda b, h, i: (b * nq + i, h)),
            pl.BlockSpec((S, bw), lambda b, h, i: (b, nb + h)),
            pl.BlockSpec((S, bw), lambda b, h, i: (b, 2 * nb + h)),
            _const_spec((4, HEAD_DIM)),
            _const_spec((1, 2 * HEAD_DIM)),
        ],
        out_specs=pl.BlockSpec((tq, bw), lambda b, h, i: (b * nq + i, h)),
        out_shape=jax.ShapeDtypeStruct((T, W), BF16),
        scratch_shapes=[pltpu.VMEM((2 * hb, tq, 1), F32), pltpu.VMEM((2 * hb, tq, 2 * LANES), F32)],
        compiler_params=_params("parallel", "parallel", "arbitrary"),
        name="diff_attn",
    )(nslopes, proj, proj, proj, lam_vec, subln.reshape(1, -1))


def _dil_attn_kernel(q_ref, kp_ref, kc_ref, vp_ref, vc_ref, o_ref, lse_ref, *, dil, slopes):
    i = pl.program_id(2)
    tq = DIL_SPAN
    qi = lax.broadcasted_iota(jnp.int32, (tq, 2 * tq), 0)
    kj = lax.broadcasted_iota(jnp.int32, (tq, 2 * tq), 1)
    delta = qi - kj + DIL_SPAN
    first_key = jnp.where(i > 0, 0, tq)
    valid = (delta >= 0) & (delta <= DIL_SPAN) & (kj >= first_key)
    deltaf = delta.astype(F32)
    lane = lax.broadcasted_iota(jnp.int32, (tq, LANES), 1)
    low = lane < HEAD_DIM
    ones = jnp.ones((2 * tq, LANES), BF16)
    for hp in range(DIL_HEADS // 2):
        cs = slice(hp * LANES, (hp + 1) * LANES)
        q = q_ref[:, cs] * jnp.asarray(QK_SCALE, BF16)
        zero = jnp.zeros_like(q)
        k = jnp.concatenate([kp_ref[:, cs], kc_ref[:, cs]], axis=0)
        v1 = jnp.concatenate([jnp.concatenate([vp_ref[:, cs], vc_ref[:, cs]], axis=0), ones], axis=1)
        res, lses = [], []
        for hh in range(2):
            qh = jnp.where(jnp.logical_not(low) if hh else low, q, zero)
            s = lax.dot_general(qh, k, _NT, preferred_element_type=F32)
            s = jnp.where(valid, s + deltaf * float(-slopes[2 * hp + hh] * dil), NEG_INF)
            m = jnp.max(s, axis=1, keepdims=True)
            r = _dot(jnp.exp(s - m).astype(BF16), v1)
            res.append(r[:, :LANES] / r[:, LANES:])
            lses.append(m + jnp.log(r[:, LANES:]))
        o_ref[:, cs] = jnp.where(low, res[0], res[1]).astype(o_ref.dtype)
        lse_ref[:, cs] = jnp.where(low, lses[0], lses[1])


def _dil_attn(proj, slopes, B, S, dil, in_width):
    T = B * S
    tq = DIL_SPAN
    n = S // dil
    nq = n // tq
    W = DIL_HEADS * HEAD_DIM
    cpr = in_width // W
    qcol = 0
    view = proj
    assert view.shape == (T // dil, dil * in_width)
    kv_prev = lambda b, r, i: b * nq + jnp.maximum(i - 1, 0)
    in_specs = [
        pl.BlockSpec((tq, W), lambda b, r, i: (b * nq + i, r * cpr + qcol)),
        pl.BlockSpec((tq, W), lambda b, r, i: (kv_prev(b, r, i), r * cpr + qcol + 1)),
        pl.BlockSpec((tq, W), lambda b, r, i: (b * nq + i, r * cpr + qcol + 1)),
        pl.BlockSpec((tq, W), lambda b, r, i: (kv_prev(b, r, i), r * cpr + qcol + 2)),
        pl.BlockSpec((tq, W), lambda b, r, i: (b * nq + i, r * cpr + qcol + 2)),
    ]
    out_spec = pl.BlockSpec((tq, W), lambda b, r, i: (b * nq + i, r))
    o, lse = pl.pallas_call(
        functools.partial(_dil_attn_kernel, dil=dil, slopes=tuple(float(s) for s in slopes)),
        grid=(B, dil, nq),
        in_specs=in_specs,
        out_specs=[out_spec, out_spec],
        out_shape=[jax.ShapeDtypeStruct((T // dil, dil * W), BF16), jax.ShapeDtypeStruct((T // dil, dil * W), F32)],
        compiler_params=_params("parallel", "parallel", "arbitrary"),
        name=f"dil_attn_d{dil}",
    )(view, view, view, view, view)
    return o.reshape(T, W), lse.reshape(T, W)


def _sb_attn_kernel(q_ref, k_ref, v_ref, o_ref, c_sc, acc_sc, *, tq):
    i = pl.program_id(2)
    ii = lax.broadcasted_iota(jnp.int32, (tq, tq), 0)
    jj = lax.broadcasted_iota(jnp.int32, (tq, tq), 1)
    strict = ii > jj
    upper = jnp.where(strict, 1.0, 0.0).astype(BF16)
    q = q_ref[...] * jnp.asarray(QK_SCALE, BF16)
    lane = lax.broadcasted_iota(jnp.int32, q.shape, 1)
    zero = jnp.zeros_like(q)

    nh = q.shape[1] // HEAD_DIM
    head_of_lane = lane // HEAD_DIM
    qhs = tuple(jnp.where(head_of_lane == hh, q, zero) for hh in range(nh))

    def tile(j, masked):
        r0 = pl.multiple_of(j * tq, tq)
        k = k_ref[pl.ds(r0, tq), :]
        v = v_ref[pl.ds(r0, tq), :]
        for hh in range(nh):
            z = lax.dot_general(qhs[hh], k, _NT, preferred_element_type=F32)
            sp = jnp.maximum(z, 0.0) + jnp.log(1.0 + jnp.exp(-jnp.abs(z)))
            log_1m = -sp
            if masked:
                log_1m = jnp.where(strict, log_1m, 0.0)
            hi = log_1m.astype(BF16)
            lo = (log_1m - hi.astype(F32)).astype(BF16)
            rem = _dot(hi, upper) + _dot(lo, upper)
            c = c_sc[hh]
            a = jnp.exp(z - sp + rem + c)
            if masked:
                a = jnp.where(strict, a, 0.0)
            acc_sc[hh] = acc_sc[hh] + _dot(a.astype(BF16), v)
            c_sc[hh] = c + jnp.sum(log_1m, axis=1, keepdims=True)

    c_sc[...] = jnp.zeros(c_sc.shape, F32)
    acc_sc[...] = jnp.zeros(acc_sc.shape, F32)
    tile(i, True)

    def cond(j):
        return jnp.logical_and(j >= 0, jnp.max(c_sc[...]) >= SB_SKIP_LOG)

    def body(j):
        tile(j, False)
        return j - 1

    lax.while_loop(cond, body, i - 1)

    out = acc_sc[nh - 1]
    for hh in range(nh - 2, -1, -1):
        out = jnp.where(head_of_lane == hh, acc_sc[hh], out)
    o_ref[...] = out.astype(o_ref.dtype)


def _sb_attn(proj, B, S, tq=256, bw=2 * LANES):
    T = B * S
    nq = S // tq
    W = SB_HEADS * HEAD_DIM
    nb = W // bw
    nh = bw // HEAD_DIM
    return pl.pallas_call(
        functools.partial(_sb_attn_kernel, tq=tq),
        grid=(B, nb, nq),
        in_specs=[
            pl.BlockSpec((tq, bw), lambda b, h, i: (b * nq + i, h)),
            pl.BlockSpec((S, bw), lambda b, h, i: (b, nb + h)),
            pl.BlockSpec((S, bw), lambda b, h, i: (b, 2 * nb + h)),
        ],
        out_specs=pl.BlockSpec((tq, bw), lambda b, h, i: (b * nq + i, h)),
        out_shape=jax.ShapeDtypeStruct((T, W), BF16),
        scratch_shapes=[pltpu.VMEM((nh, tq, 1), F32), pltpu.VMEM((nh, tq, bw), F32)],
        compiler_params=_params("parallel", "parallel", "arbitrary"),
        name="sb_attn",
    )(proj, proj, proj)


def _merge_kernel(x_ref, g_ref, ya_ref, o1_ref, o2_ref, o3_ref, l1_ref, l2_ref, l3_ref, yc_ref,
                  wg_ref, bg_ref, wb_ref, wo_ref, out_ref, *, n_chunk):
    x = x_ref[...]
    D = x.shape[1]
    h = _rms(x, g_ref[...]).astype(BF16)
    l1, l2, l3 = l1_ref[...], l2_ref[...], l3_ref[...]
    mx = jnp.maximum(jnp.maximum(l1, l2), l3)
    w1, w2, w3 = jnp.exp(l1 - mx), jnp.exp(l2 - mx), jnp.exp(l3 - mx)
    yb = ((w1 * o1_ref[...] + w2 * o2_ref[...] + w3 * o3_ref[...]) / (w1 + w2 + w3)).astype(BF16)
    ys = (ya_ref[...], yb, yc_ref[...])
    parts = []
    for c0 in range(0, D, n_chunk):
        m = jnp.zeros((x.shape[0], n_chunk), F32)
        for i in range(3):
            gc = slice(i * D + c0, i * D + c0 + n_chunk)
            gate = _sigmoid(_dot(h, wg_ref[:, gc]) + bg_ref[:, gc])
            m = m + gate * _dot(ys[i], wb_ref[i, :, c0:c0 + n_chunk])
        parts.append(m.astype(BF16))
    out_ref[...] = x + _dot(jnp.concatenate(parts, axis=1), wo_ref[...])


def _merge(x, g, ya, dil_outs, yc, w_gate, b_gate, w_branch, w_out, tm=256, n_chunk=512):
    T, D = x.shape
    W = ya.shape[1]
    row = lambda i: (i, 0)
    ysp = pl.BlockSpec((tm, W), row)
    (o1, s1), (o2, s2), (o3, s3) = dil_outs
    return pl.pallas_call(
        functools.partial(_merge_kernel, n_chunk=n_chunk),
        grid=(T // tm,),
        in_specs=[pl.BlockSpec((tm, D), row), _const_spec((1, D)), ysp, ysp, ysp, ysp, ysp, ysp, ysp, ysp,
                  _const_spec(w_gate.shape), _const_spec((1, 3 * D)), _const_spec(w_branch.shape),
                  _const_spec(w_out.shape)],
        out_specs=pl.BlockSpec((tm, D), row),
        out_shape=jax.ShapeDtypeStruct((T, D), F32),
        compiler_params=_params("parallel"),
        name="merge",
    )(x, g.reshape(1, D), ya, o1, o2, o3, s1, s2, s3, yc, w_gate, b_gate.reshape(1, -1), w_branch, w_out)


def _ffn_dense_kernel(x_ref, g_ref, wg_ref, wu_ref, wd_ref, gf_ref, out_ref, *, f_chunk, final):
    x = x_ref[...]
    h = _rms(x, g_ref[...]).astype(BF16)
    acc = x
    for f0 in range(0, wg_ref.shape[1], f_chunk):
        a = _dot(h, wg_ref[:, f0:f0 + f_chunk])
        u = _dot(h, wu_ref[:, f0:f0 + f_chunk])
        acc = acc + _dot((a * _sigmoid(a) * u).astype(BF16), wd_ref[f0:f0 + f_chunk, :])
    out_ref[...] = _rms(acc, gf_ref[...]) if final else acc


def _ffn_dense(x, g, w_g, w_u, w_d, g_final, final, tm=512, f_chunk=256):
    T, D = x.shape
    row = lambda i: (i, 0)
    return pl.pallas_call(
        functools.partial(_ffn_dense_kernel, f_chunk=f_chunk, final=final),
        grid=(T // tm,),
        in_specs=[pl.BlockSpec((tm, D), row), _const_spec((1, D)), _const_spec(w_g.shape), _const_spec(w_u.shape),
                  _const_spec(w_d.shape), _const_spec((1, D))],
        out_specs=pl.BlockSpec((tm, D), row),
        out_shape=jax.ShapeDtypeStruct((T, D), F32),
        compiler_params=_params("parallel"),
        name="ffn_dense",
    )(x, g.reshape(1, D), w_g, w_u, w_d, g_final.reshape(1, D))


def _moe_kernel(x_ref, g_ref, wr_ref, wg_ref, wu_ref, wd_ref, gf_ref, out_ref, h_sc, gate_sc, acc_sc, *, final):
    e = pl.program_id(1)
    f = pl.program_id(2)
    lane = lax.broadcasted_iota(jnp.int32, gate_sc.shape, 1)

    @pl.when(jnp.logical_and(e == 0, f == 0))
    def _():
        x = x_ref[...]
        hf = _rms(x, g_ref[...])
        h_sc[...] = hf.astype(BF16)
        logits = jnp.dot(hf, wr_ref[...], preferred_element_type=F32, precision=lax.Precision.HIGHEST)
        logits = jnp.where(lane < N_EXPERTS, logits, NEG_INF)
        v1 = jnp.max(logits, axis=1, keepdims=True)
        i1 = jnp.min(jnp.where(logits == v1, lane, LANES), axis=1, keepdims=True)
        rest = jnp.where(lane == i1, NEG_INF, logits)
        v2 = jnp.max(rest, axis=1, keepdims=True)
        i2 = jnp.min(jnp.where(rest == v2, lane, LANES), axis=1, keepdims=True)
        e2 = jnp.exp(v2 - v1)
        gate_sc[...] = jnp.where(lane == i1, 1.0 / (1.0 + e2), jnp.where(lane == i2, e2 / (1.0 + e2), 0.0))
        acc_sc[...] = x

    h = h_sc[...]
    a = _dot(h, wg_ref[0])
    u = _dot(h, wu_ref[0])
    ge = jnp.sum(jnp.where(lane == e, gate_sc[...], 0.0), axis=1, keepdims=True)
    acc_sc[...] += _dot((a * _sigmoid(a) * u * ge).astype(BF16), wd_ref[0])

    @pl.when(jnp.logical_and(e == pl.num_programs(1) - 1, f == pl.num_programs(2) - 1))
    def _():
        acc = acc_sc[...]
        out_ref[...] = _rms(acc, gf_ref[...]) if final else acc


def _moe(x, g, w_router, w_g, w_u, w_d, g_final, final, tm=512, tf=1792):
    T, D = x.shape
    E, _, F = w_g.shape
    wr = jnp.zeros((D, LANES), F32).at[:, :E].set(w_router)
    row = lambda i, e, f: (i, 0)
    return pl.pallas_call(
        functools.partial(_moe_kernel, final=final),
        grid=(T // tm, E, F // tf),
        in_specs=[pl.BlockSpec((tm, D), row), _const_spec((1, D)), _const_spec((D, LANES)),
                  pl.BlockSpec((1, D, tf), lambda i, e, f: (e, 0, f)),
                  pl.BlockSpec((1, D, tf), lambda i, e, f: (e, 0, f)),
                  pl.BlockSpec((1, tf, D), lambda i, e, f: (e, f, 0)),
                  _const_spec((1, D))],
        out_specs=pl.BlockSpec((tm, D), row),
        out_shape=jax.ShapeDtypeStruct((T, D), F32),
        scratch_shapes=[pltpu.VMEM((tm, D), BF16), pltpu.VMEM((tm, LANES), F32), pltpu.VMEM((tm, D), F32)],
        compiler_params=_params("parallel", "arbitrary", "arbitrary"),
        name="moe",
    )(x, g.reshape(1, D), wr, w_g, w_u, w_d, g_final.reshape(1, D))


def kernel(x, norm_mix, w_in, diff_lambda, diff_subln, w_gate, b_gate, w_branch, w_out, norm_ffn, w_dense_gate,
           w_dense_up, w_dense_down, w_router, w_moe_gate, w_moe_up, w_moe_down, norm_final):
    B, S, D = x.shape
    depth = norm_mix.shape[0]
    in_width = w_in.shape[2]
    slopes_diff, slopes_dil = _alibi_slopes()
    nslopes_diff = jnp.asarray(-slopes_diff, F32)
    xf = x.reshape(B * S, D)
    for layer in range(depth):
        dils = tuple(dil for _, dil in DIL_PATTERNS if dil > 1)
        pa, pb, pc, *pbd = _norm_proj(xf, norm_mix[layer], w_in[layer].astype(BF16), dils)
        views = {1: pb, **dict(zip(dils, pbd))}
        lam_init = 0.8 - 0.6 * math.exp(-0.3 * layer)
        ya = _diff_attn(pa, diff_lambda[layer], diff_subln[layer], nslopes_diff, B, S, lam_init)
        dil_outs = [_dil_attn(views[dil], slopes_dil, B, S, dil, in_width // 3) for _, dil in DIL_PATTERNS]
        yc = _sb_attn(pc, B, S)
        xf = _merge(xf, norm_mix[layer], ya, dil_outs, yc, w_gate[layer].astype(BF16), b_gate[layer],
                    w_branch[layer].astype(BF16), w_out[layer].astype(BF16))
        final = layer == depth - 1
        j = layer // 2
        if layer % 2 == 0:
            xf = _ffn_dense(xf, norm_ffn[layer], w_dense_gate[j].astype(BF16), w_dense_up[j].astype(BF16),
                            w_dense_down[j].astype(BF16), norm_final, final)
        else:
            xf = _moe(xf, norm_ffn[layer], w_router[j], w_moe_gate[j].astype(BF16), w_moe_up[j].astype(BF16),
                      w_moe_down[j].astype(BF16), norm_final, final)
    return xf.reshape(B, S, D)
```

```python
import functools
import math

import numpy as np
import jax
import jax.numpy as jnp
from jax import lax
from jax.experimental import pallas as pl
from jax.experimental.pallas import tpu as pltpu

F32 = jnp.float32
BF16 = jnp.bfloat16

HEAD_DIM = 64
QK_SCALE = HEAD_DIM ** -0.5
DIFF_HEADS = 4
DIL_HEADS = 8
SB_HEADS = 8
DIL_PATTERNS = ((128, 1), (512, 4), (2048, 16))
DIL_SPAN = 128
DIL_SUBTILES = 4
N_EXPERTS = 8
NORM_EPS = 1e-6
NEG_INF = -1e30
SB_SKIP_LOG = -104.0
LANES = 128
VMEM_LIMIT = 56 * 1024 * 1024

_NT = (((1,), (1,)), ((), ()))


def _alibi_slopes():
    n = DIFF_HEADS + DIL_HEADS
    slopes = (2.0 ** (-8.0 * np.arange(1, n + 1) / n)).astype(np.float32)
    is_diff = (np.arange(n) % 3) == 2
    return slopes[is_diff], slopes[~is_diff]


def _rms(x, g):
    ms = jnp.mean(x * x, axis=-1, keepdims=True)
    return x * lax.rsqrt(ms + NORM_EPS) * g


def _dot(a, b):
    return jnp.dot(a, b, preferred_element_type=F32)


def _sigmoid(a):
    return 1.0 / (1.0 + jnp.exp(-a))


def _params(*sem):
    return pltpu.CompilerParams(dimension_semantics=sem, vmem_limit_bytes=VMEM_LIMIT)


def _const_spec(shape):
    nd = len(shape)
    return pl.BlockSpec(shape, lambda *_: (0,) * nd)


def _norm_proj_kernel(x_ref, g_ref, w_ref, pa_ref, pb_ref, pc_ref, *rest, n_chunk, dils):
    pbd_refs, res_sc = rest[:-1], rest[-1]
    h = _rms(x_ref[...], g_ref[...]).astype(BF16)
    tm = h.shape[0]
    N = pa_ref.shape[1]
    for n0 in range(0, N, n_chunk):
        pa_ref[:, n0:n0 + n_chunk] = _dot(h, w_ref[:, n0:n0 + n_chunk]).astype(pa_ref.dtype)
        pc_ref[:, n0:n0 + n_chunk] = _dot(h, w_ref[:, 2 * N + n0:2 * N + n0 + n_chunk]).astype(pc_ref.dtype)
        res = _dot(h, w_ref[:, N + n0:N + n0 + n_chunk])
        pb_ref[:, n0:n0 + n_chunk] = res.astype(pb_ref.dtype)
        for g in range(n_chunk // LANES):
            res_sc[g] = res[:, g * LANES:(g + 1) * LANES]
        for d, o_ref in zip(dils, pbd_refs):
            for r in range(d):
                for g in range(n_chunk // LANES):
                    c0 = r * N + n0 + g * LANES
                    o_ref[:, c0:c0 + LANES] = res_sc[g, pl.ds(r, tm // d, stride=d), :].astype(o_ref.dtype)


def _norm_proj(x, g, w, dils, tm=512, n_chunk=512):
    T, D = x.shape
    N = w.shape[1] // 3
    row = lambda i: (i, 0)
    return pl.pallas_call(
        functools.partial(_norm_proj_kernel, n_chunk=n_chunk, dils=dils),
        grid=(T // tm,),
        in_specs=[pl.BlockSpec((tm, D), row), _const_spec((1, D)), _const_spec(w.shape)],
        out_specs=[pl.BlockSpec((tm, N), row)] * 3 + [pl.BlockSpec((tm // d, d * N), row) for d in dils],
        out_shape=[jax.ShapeDtypeStruct((T, N), BF16)] * 3
                  + [jax.ShapeDtypeStruct((T // d, d * N), BF16) for d in dils],
        scratch_shapes=[pltpu.VMEM((n_chunk // LANES, tm, LANES), F32)],
        compiler_params=_params("parallel"),
        name="norm_proj",
    )(x, g.reshape(1, D), w)


def _diff_attn_kernel(nslope_ref, q_ref, k_ref, v_ref, lam_ref, subln_ref, o_ref, m_sc, acc_sc, *, tq, lam_init):
    i = pl.program_id(2)
    hb = q_ref.shape[1] // LANES
    nslopes = [nslope_ref[pl.program_id(1) * hb + g] for g in range(hb)]
    ii = lax.broadcasted_iota(jnp.int32, (tq, tq), 0)
    jj = lax.broadcasted_iota(jnp.int32, (tq, tq), 1)
    dist = (ii - jj).astype(F32)
    rels = [dist * ns for ns in nslopes]
    causal = ii >= jj
    q = q_ref[...] * jnp.asarray(QK_SCALE, BF16)
    half_of_lane = lax.broadcasted_iota(jnp.int32, q.shape, 1) // HEAD_DIM
    zero = jnp.zeros_like(q)
    qs = [jnp.where(half_of_lane == n, q, zero) for n in range(2 * hb)]
    m_sc[...] = jnp.full(m_sc.shape, NEG_INF, F32)
    acc_sc[...] = jnp.zeros(acc_sc.shape, F32)

    def tile(j, masked):
        r0 = pl.multiple_of(j * tq, tq)
        k = k_ref[pl.ds(r0, tq), :]
        v = v_ref[pl.ds(r0, tq), :]
        ones = jnp.ones((tq, LANES), BF16)
        for g in range(hb):
            v1 = jnp.concatenate([v[:, g * LANES:(g + 1) * LANES], ones], axis=1)
            off = jnp.full((tq, 1), (i - j) * tq, jnp.int32).astype(F32) * nslopes[g]
            for c in range(2):
                n = 2 * g + c
                s = lax.dot_general(qs[n], k, _NT, preferred_element_type=F32)
                t = s + rels[g]
                if masked:
                    t = jnp.where(causal, t, NEG_INF)
                m_prev = m_sc[n]
                m_new = jnp.maximum(m_prev, jnp.max(t, axis=1, keepdims=True) + off)
                alpha = jnp.exp(m_prev - m_new)
                p = jnp.exp(t - (m_new - off))
                acc_sc[n] = alpha * acc_sc[n] + _dot(p.astype(BF16), v1)
                m_sc[n] = m_new

    def body(j, carry):
        tile(j, False)
        return carry

    lax.fori_loop(0, i, body, 0)
    tile(i, True)

    lv = lam_ref[...]
    lam = (jnp.exp(jnp.sum(lv[0:1] * lv[1:2], axis=1, keepdims=True))
           - jnp.exp(jnp.sum(lv[2:3] * lv[3:4], axis=1, keepdims=True)) + lam_init)
    for g in range(hb):
        a0, a1 = acc_sc[2 * g], acc_sc[2 * g + 1]
        o = a0[:, :LANES] / a0[:, LANES:] - lam * (a1[:, :LANES] / a1[:, LANES:])
        o_ref[:, g * LANES:(g + 1) * LANES] = (_rms(o, subln_ref[...]) * (1.0 - lam_init)).astype(o_ref.dtype)


def _diff_attn(proj, lam_vec, subln, nslopes, B, S, lam_init, tq=512, hb=2):
    T = B * S
    nq = S // tq
    W = DIFF_HEADS * 2 * HEAD_DIM
    nb = DIFF_HEADS // hb
    bw = hb * LANES
    return pl.pallas_call(
        functools.partial(_diff_attn_kernel, tq=tq, lam_init=lam_init),
        grid=(B, nb, nq),
        in_specs=[
            pl.BlockSpec(memory_space=pltpu.SMEM),
            pl.BlockSpec((tq, bw), lambda b, h, i: (b * nq + i, h)),
            pl.BlockSpec((S, bw), lambda b, h, i: (b, nb + h)),
            pl.BlockSpec((S, bw), lambda b, h, i: (b, 2 * nb + h)),
            _const_spec((4, HEAD_DIM)),
            _const_spec((1, 2 * HEAD_DIM)),
        ],
        out_specs=pl.BlockSpec((tq, bw), lambda b, h, i: (b * nq + i, h)),
        out_shape=jax.ShapeDtypeStruct((T, W), BF16),
        scratch_shapes=[pltpu.VMEM((2 * hb, tq, 1), F32), pltpu.VMEM((2 * hb, tq, 2 * LANES), F32)],
        compiler_params=_params("parallel", "parallel", "arbitrary"),
        name="diff_attn",
    )(nslopes, proj, proj, proj, lam_vec, subln.reshape(1, -1))


def _dil_attn_kernel(q_ref, kp_ref, kc_ref, vp_ref, vc_ref, o_ref, lse_ref, *, dil, slopes):
    i = pl.program_id(2)
    tq = DIL_SPAN
    qi = lax.broadcasted_iota(jnp.int32, (tq, 2 * tq), 0)
    kj = lax.broadcasted_iota(jnp.int32, (tq, 2 * tq), 1)
    delta = qi - kj + DIL_SPAN
    first_key = jnp.where(i > 0, 0, tq)
    in_window = (delta >= 0) & (delta <= DIL_SPAN)
    deltaf = delta.astype(F32)
    lane = lax.broadcasted_iota(jnp.int32, (tq, LANES), 1)
    low = lane < HEAD_DIM
    ones = jnp.ones((2 * tq, LANES), BF16)
    for hp in range(DIL_HEADS // 2):
        cs = slice(hp * LANES, (hp + 1) * LANES)
        for sub in range(q_ref.shape[0] // tq):
            rows = slice(sub * tq, (sub + 1) * tq)
            q = q_ref[rows, cs] * jnp.asarray(QK_SCALE, BF16)
            zero = jnp.zeros_like(q)
            if sub == 0:
                valid = in_window & (kj >= first_key)
                k = jnp.concatenate([kp_ref[:, cs], kc_ref[rows, cs]], axis=0)
                v = jnp.concatenate([vp_ref[:, cs], vc_ref[rows, cs]], axis=0)
            else:
                valid = in_window
                keys = slice((sub - 1) * tq, (sub + 1) * tq)
                k, v = kc_ref[keys, cs], vc_ref[keys, cs]
            v1 = jnp.concatenate([v, ones], axis=1)
            res, lses = [], []
            for hh in range(2):
                qh = jnp.where(jnp.logical_not(low) if hh else low, q, zero)
                s = lax.dot_general(qh, k, _NT, preferred_element_type=F32)
                s = jnp.where(valid, s + deltaf * float(-slopes[2 * hp + hh] * dil), NEG_INF)
                m = jnp.max(s, axis=1, keepdims=True)
                r = _dot(jnp.exp(s - m).astype(BF16), v1)
                res.append(r[:, :LANES] / r[:, LANES:])
                lses.append(m + jnp.log(r[:, LANES:]))
            o_ref[rows, cs] = jnp.where(low, res[0], res[1]).astype(o_ref.dtype)
            lse_ref[rows, cs] = jnp.where(low, lses[0], lses[1])


def _dil_attn(proj, slopes, B, S, dil, in_width):
    T = B * S
    tq = DIL_SPAN
    n = S // dil
    nq = n // tq
    W = DIL_HEADS * HEAD_DIM
    cpr = in_width // W
    qcol = 0
    view = proj
    assert view.shape == (T // dil, dil * in_width)
    nsub = min(DIL_SUBTILES, nq)
    tb = nsub * tq
    nq = nq // nsub
    kv_prev = lambda b, r, i: jnp.maximum((b * nq + i) * nsub - 1, 0)
    in_specs = [
        pl.BlockSpec((tb, W), lambda b, r, i: (b * nq + i, r * cpr + qcol)),
        pl.BlockSpec((tq, W), lambda b, r, i: (kv_prev(b, r, i), r * cpr + qcol + 1)),
        pl.BlockSpec((tb, W), lambda b, r, i: (b * nq + i, r * cpr + qcol + 1)),
        pl.BlockSpec((tq, W), lambda b, r, i: (kv_prev(b, r, i), r * cpr + qcol + 2)),
        pl.BlockSpec((tb, W), lambda b, r, i: (b * nq + i, r * cpr + qcol + 2)),
    ]
    out_spec = pl.BlockSpec((tb, W), lambda b, r, i: (b * nq + i, r))
    o, lse = pl.pallas_call(
        functools.partial(_dil_attn_kernel, dil=dil, slopes=tuple(float(s) for s in slopes)),
        grid=(B, dil, nq),
        in_specs=in_specs,
        out_specs=[out_spec, out_spec],
        out_shape=[jax.ShapeDtypeStruct((T // dil, dil * W), BF16), jax.ShapeDtypeStruct((T // dil, dil * W), F32)],
        compiler_params=_params("parallel", "parallel", "arbitrary"),
        name=f"dil_attn_d{dil}",
    )(view, view, view, view, view)
    return o.reshape(T, W), lse.reshape(T, W)


def _sb_attn_kernel(q_ref, k_ref, v_ref, o_ref, c_sc, acc_sc, *, tq):
    i = pl.program_id(2)
    ii = lax.broadcasted_iota(jnp.int32, (tq, tq), 0)
    jj = lax.broadcasted_iota(jnp.int32, (tq, tq), 1)
    strict = ii > jj
    upper = jnp.where(strict, 1.0, 0.0).astype(BF16)
    q = q_ref[...] * jnp.asarray(QK_SCALE, BF16)
    lane = lax.broadcasted_iota(jnp.int32, q.shape, 1)
    zero = jnp.zeros_like(q)

    nh = q.shape[1] // HEAD_DIM
    head_of_lane = lane // HEAD_DIM
    qhs = tuple(jnp.where(head_of_lane == hh, q, zero) for hh in range(nh))

    def tile(j, masked):
        r0 = pl.multiple_of(j * tq, tq)
        k = k_ref[pl.ds(r0, tq), :]
        v = v_ref[pl.ds(r0, tq), :]
        for hh in range(nh):
            z = lax.dot_general(qhs[hh], k, _NT, preferred_element_type=F32)
            sp = jnp.maximum(z, 0.0) + jnp.log(1.0 + jnp.exp(-jnp.abs(z)))
            log_1m = -sp
            if masked:
                log_1m = jnp.where(strict, log_1m, 0.0)
            hi = log_1m.astype(BF16)
            lo = (log_1m - hi.astype(F32)).astype(BF16)
            rem = _dot(hi, upper) + _dot(lo, upper)
            c = c_sc[hh]
            a = jnp.exp(z - sp + rem + c)
            if masked:
                a = jnp.where(strict, a, 0.0)
            acc_sc[hh] = acc_sc[hh] + _dot(a.astype(BF16), v)
            c_sc[hh] = c + jnp.sum(log_1m, axis=1, keepdims=True)

    c_sc[...] = jnp.zeros(c_sc.shape, F32)
    acc_sc[...] = jnp.zeros(acc_sc.shape, F32)
    tile(i, True)

    def cond(j):
        return jnp.logical_and(j >= 0, jnp.max(c_sc[...]) >= SB_SKIP_LOG)

    def body(j):
        tile(j, False)
        return j - 1

    lax.while_loop(cond, body, i - 1)

    out = acc_sc[nh - 1]
    for hh in range(nh - 2, -1, -1):
        out = jnp.where(head_of_lane == hh, acc_sc[hh], out)
    o_ref[...] = out.astype(o_ref.dtype)


def _sb_attn(proj, B, S, tq=256, bw=2 * LANES):
    T = B * S
    nq = S // tq
    W = SB_HEADS * HEAD_DIM
    nb = W // bw
    nh = bw // HEAD_DIM
    return pl.pallas_call(
        functools.partial(_sb_attn_kernel, tq=tq),
        grid=(B, nb, nq),
        in_specs=[
            pl.BlockSpec((tq, bw), lambda b, h, i: (b * nq + i, h)),
            pl.BlockSpec((S, bw), lambda b, h, i: (b, nb + h)),
            pl.BlockSpec((S, bw), lambda b, h, i: (b, 2 * nb + h)),
        ],
        out_specs=pl.BlockSpec((tq, bw), lambda b, h, i: (b * nq + i, h)),
        out_shape=jax.ShapeDtypeStruct((T, W), BF16),
        scratch_shapes=[pltpu.VMEM((nh, tq, 1), F32), pltpu.VMEM((nh, tq, bw), F32)],
        compiler_params=_params("parallel", "parallel", "arbitrary"),
        name="sb_attn",
    )(proj, proj, proj)


def _merge_kernel(x_ref, g_ref, ya_ref, o1_ref, o2_ref, o3_ref, l1_ref, l2_ref, l3_ref, yc_ref,
                  wg_ref, bg_ref, wb_ref, wo_ref, out_ref, *, n_chunk):
    x = x_ref[...]
    D = x.shape[1]
    h = _rms(x, g_ref[...]).astype(BF16)
    l1, l2, l3 = l1_ref[...], l2_ref[...], l3_ref[...]
    mx = jnp.maximum(jnp.maximum(l1, l2), l3)
    w1, w2, w3 = jnp.exp(l1 - mx), jnp.exp(l2 - mx), jnp.exp(l3 - mx)
    yb = ((w1 * o1_ref[...] + w2 * o2_ref[...] + w3 * o3_ref[...]) / (w1 + w2 + w3)).astype(BF16)
    ys = (ya_ref[...], yb, yc_ref[...])
    parts = []
    for c0 in range(0, D, n_chunk):
        m = jnp.zeros((x.shape[0], n_chunk), F32)
        for i in range(3):
            gc = slice(i * D + c0, i * D + c0 + n_chunk)
            gate = _sigmoid(_dot(h, wg_ref[:, gc]) + bg_ref[:, gc])
            m = m + gate * _dot(ys[i], wb_ref[i, :, c0:c0 + n_chunk])
        parts.append(m.astype(BF16))
    out_ref[...] = x + _dot(jnp.concatenate(parts, axis=1), wo_ref[...])


def _merge(x, g, ya, dil_outs, yc, w_gate, b_gate, w_branch, w_out, tm=256, n_chunk=512):
    T, D = x.shape
    W = ya.shape[1]
    row = lambda i: (i, 0)
    ysp = pl.BlockSpec((tm, W), row)
    (o1, s1), (o2, s2), (o3, s3) = dil_outs
    return pl.pallas_call(
        functools.partial(_merge_kernel, n_chunk=n_chunk),
        grid=(T // tm,),
        in_specs=[pl.BlockSpec((tm, D), row), _const_spec((1, D)), ysp, ysp, ysp, ysp, ysp, ysp, ysp, ysp,
                  _const_spec(w_gate.shape), _const_spec((1, 3 * D)), _const_spec(w_branch.shape),
                  _const_spec(w_out.shape)],
        out_specs=pl.BlockSpec((tm, D), row),
        out_shape=jax.ShapeDtypeStruct((T, D), F32),
        compiler_params=_params("parallel"),
        name="merge",
    )(x, g.reshape(1, D), ya, o1, o2, o3, s1, s2, s3, yc, w_gate, b_gate.reshape(1, -1), w_branch, w_out)


def _ffn_dense_kernel(x_ref, g_ref, wg_ref, wu_ref, wd_ref, gf_ref, out_ref, *, f_chunk, final):
    x = x_ref[...]
    h = _rms(x, g_ref[...]).astype(BF16)
    acc = x
    for f0 in range(0, wg_ref.shape[1], f_chunk):
        a = _dot(h, wg_ref[:, f0:f0 + f_chunk])
        u = _dot(h, wu_ref[:, f0:f0 + f_chunk])
        acc = acc + _dot((a * _sigmoid(a) * u).astype(BF16), wd_ref[f0:f0 + f_chunk, :])
    out_ref[...] = _rms(acc, gf_ref[...]) if final else acc


def _ffn_dense(x, g, w_g, w_u, w_d, g_final, final, tm=512, f_chunk=256):
    T, D = x.shape
    row = lambda i: (i, 0)
    return pl.pallas_call(
        functools.partial(_ffn_dense_kernel, f_chunk=f_chunk, final=final),
        grid=(T // tm,),
        in_specs=[pl.BlockSpec((tm, D), row), _const_spec((1, D)), _const_spec(w_g.shape), _const_spec(w_u.shape),
                  _const_spec(w_d.shape), _const_spec((1, D))],
        out_specs=pl.BlockSpec((tm, D), row),
        out_shape=jax.ShapeDtypeStruct((T, D), F32),
        compiler_params=_params("parallel"),
        name="ffn_dense",
    )(x, g.reshape(1, D), w_g, w_u, w_d, g_final.reshape(1, D))


def _moe_kernel(x_ref, g_ref, wr_ref, wg_ref, wu_ref, wd_ref, gf_ref, out_ref, h_sc, gate_sc, acc_sc, *, final):
    e = pl.program_id(1)
    f = pl.program_id(2)
    lane = lax.broadcasted_iota(jnp.int32, gate_sc.shape, 1)

    @pl.when(jnp.logical_and(e == 0, f == 0))
    def _():
        x = x_ref[...]
        hf = _rms(x, g_ref[...])
        h_sc[...] = hf.astype(BF16)
        logits = jnp.dot(hf, wr_ref[...], preferred_element_type=F32, precision=lax.Precision.HIGHEST)
        logits = jnp.where(lane < N_EXPERTS, logits, NEG_INF)
        v1 = jnp.max(logits, axis=1, keepdims=True)
        i1 = jnp.min(jnp.where(logits == v1, lane, LANES), axis=1, keepdims=True)
        rest = jnp.where(lane == i1, NEG_INF, logits)
        v2 = jnp.max(rest, axis=1, keepdims=True)
        i2 = jnp.min(jnp.where(rest == v2, lane, LANES), axis=1, keepdims=True)
        e2 = jnp.exp(v2 - v1)
        gate_sc[...] = jnp.where(lane == i1, 1.0 / (1.0 + e2), jnp.where(lane == i2, e2 / (1.0 + e2), 0.0))
        acc_sc[...] = x

    h = h_sc[...]
    a = _dot(h, wg_ref[0])
    u = _dot(h, wu_ref[0])
    ge = jnp.sum(jnp.where(lane == e, gate_sc[...], 0.0), axis=1, keepdims=True)
    acc_sc[...] += _dot((a * _sigmoid(a) * u * ge).astype(BF16), wd_ref[0])

    @pl.when(jnp.logical_and(e == pl.num_programs(1) - 1, f == pl.num_programs(2) - 1))
    def _():
        acc = acc_sc[...]
        out_ref[...] = _rms(acc, gf_ref[...]) if final else acc


def _moe(x, g, w_router, w_g, w_u, w_d, g_final, final, tm=512, tf=1792):
    T, D = x.shape
    E, _, F = w_g.shape
    wr = jnp.zeros((D, LANES), F32).at[:, :E].set(w_router)
    row = lambda i, e, f: (i, 0)
    return pl.pallas_call(
        functools.partial(_moe_kernel, final=final),
        grid=(T // tm, E, F // tf),
        in_specs=[pl.BlockSpec((tm, D), row), _const_spec((1, D)), _const_spec((D, LANES)),
                  pl.BlockSpec((1, D, tf), lambda i, e, f: (e, 0, f)),
                  pl.BlockSpec((1, D, tf), lambda i, e, f: (e, 0, f)),
                  pl.BlockSpec((1, tf, D), lambda i, e, f: (e, f, 0)),
                  _const_spec((1, D))],
        out_specs=pl.BlockSpec((tm, D), row),
        out_shape=jax.ShapeDtypeStruct((T, D), F32),
        scratch_shapes=[pltpu.VMEM((tm, D), BF16), pltpu.VMEM((tm, LANES), F32), pltpu.VMEM((tm, D), F32)],
        compiler_params=_params("parallel", "arbitrary", "arbitrary"),
        name="moe",
    )(x, g.reshape(1, D), wr, w_g, w_u, w_d, g_final.reshape(1, D))


def kernel(x, norm_mix, w_in, diff_lambda, diff_subln, w_gate, b_gate, w_branch, w_out, norm_ffn, w_dense_gate,
           w_dense_up, w_dense_down, w_router, w_moe_gate, w_moe_up, w_moe_down, norm_final):
    B, S, D = x.shape
    depth = norm_mix.shape[0]
    in_width = w_in.shape[2]
    slopes_diff, slopes_dil = _alibi_slopes()
    nslopes_diff = jnp.asarray(-slopes_diff, F32)
    xf = x.reshape(B * S, D)
    for layer in range(depth):
        dils = tuple(dil for _, dil in DIL_PATTERNS if dil > 1)
        pa, pb, pc, *pbd = _norm_proj(xf, norm_mix[layer], w_in[layer].astype(BF16), dils)
        views = {1: pb, **dict(zip(dils, pbd))}
        lam_init = 0.8 - 0.6 * math.exp(-0.3 * layer)
        ya = _diff_attn(pa, diff_lambda[layer], diff_subln[layer], nslopes_diff, B, S, lam_init)
        dil_outs = [_dil_attn(views[dil], slopes_dil, B, S, dil, in_width // 3) for _, dil in DIL_PATTERNS]
        yc = _sb_attn(pc, B, S)
        xf = _merge(xf, norm_mix[layer], ya, dil_outs, yc, w_gate[layer].astype(BF16), b_gate[layer],
                    w_branch[layer].astype(BF16), w_out[layer].astype(BF16))
        final = layer == depth - 1
        j = layer // 2
        if layer % 2 == 0:
            xf = _ffn_dense(xf, norm_ffn[layer], w_dense_gate[j].astype(BF16), w_dense_up[j].astype(BF16),
                            w_dense_down[j].astype(BF16), norm_final, final)
        else:
            xf = _moe(xf, norm_ffn[layer], w_router[j], w_moe_gate[j].astype(BF16), w_moe_up[j].astype(BF16),
                      w_moe_down[j].astype(BF16), norm_final, final)
    return xf.reshape(B, S, D)
```
